```python
import functools
import jax
import jax.numpy as jnp
from jax import lax
import numpy as np

D_MODEL = 1024
BATCH = 4
SEQ = 4096
DEPTH = 1
DEC_BATCH = 32
DEC_SEQ = 8
PAST_LEN = 16384
PAGE_SIZE = 128

HEAD_DIM = 64
N_HEADS_A = 8
N_HEADS_B = 8
WIDTH_A = N_HEADS_A * HEAD_DIM
WIDTH_B = N_HEADS_B * HEAD_DIM
N_IDX_HEADS = 8
IDX_DIM = 64
TOPK_MAX = 256
Q_BLOCK = 128
ROPE_THETA = 10000.0
N_EXPERTS = 256
EXPERT_TOP_K = 8
N_EXPERT_GROUPS = 8
TOPK_GROUPS = 4
D_EXPERT = 256
D_SHARED = 256
ROUTED_SCALE = 2.5
MOE_MAX_BLOCK_ROWS = 128
MOE_MIN_BLOCK_ROWS = 8
RMS_EPS = 1e-6
IN_SIZES = (WIDTH_A, WIDTH_A, WIDTH_A, N_IDX_HEADS * IDX_DIM, IDX_DIM, N_IDX_HEADS,
            WIDTH_B, WIDTH_B, WIDTH_B, D_MODEL, D_MODEL)
IN_COLS = sum(IN_SIZES)
F32 = jnp.float32

kernel_name = 'hybrid_dsa_stickbreak_moe_step'


def _rms_norm(x, g):
    xf = x.astype(F32)
    y = xf * lax.rsqrt(jnp.mean(xf * xf, axis=-1, keepdims=True) + RMS_EPS)
    return (y * g.astype(F32)).astype(x.dtype)


def _rope(x, pos):
    half = x.shape[-1] // 2
    inv_freq = ROPE_THETA ** (-jnp.arange(half, dtype=F32) / half)
    ang = pos.astype(F32)[:, None] * inv_freq[None, :]
    cos = jnp.cos(ang)[None, :, None, :]
    sin = jnp.sin(ang)[None, :, None, :]
    xf = x.astype(F32)
    x1, x2 = xf[..., :half], xf[..., half:]
    return jnp.concatenate([x1 * cos - x2 * sin, x2 * cos + x1 * sin], axis=-1).astype(x.dtype)


def _to_blocks(x):
    b, s = x.shape[:2]
    return jnp.moveaxis(x.reshape(b, s // Q_BLOCK, Q_BLOCK, *x.shape[2:]), 1, 0)


def _from_blocks(x):
    nb, b, qb = x.shape[:3]
    return jnp.moveaxis(x, 0, 1).reshape(b, nb * qb, *x.shape[3:])


def _project(h, w_in, pos):
    b, t, _ = h.shape
    offsets = np.cumsum(IN_SIZES)[:-1].tolist()
    q_a, k_a, v_a, q_i, k_i, w_i, q_b, k_b, v_b, g_a, g_b = jnp.split(h @ w_in, offsets, axis=-1)
    q_a = _rope(q_a.reshape(b, t, N_HEADS_A, HEAD_DIM), pos)
    k_a = _rope(k_a.reshape(b, t, N_HEADS_A, HEAD_DIM), pos)
    v_a = v_a.reshape(b, t, N_HEADS_A, HEAD_DIM)
    q_i = _rope(q_i.reshape(b, t, N_IDX_HEADS, IDX_DIM), pos)
    k_i = _rope(k_i[:, :, None, :], pos)[:, :, 0, :]
    q_b = q_b.reshape(b, t, N_HEADS_B, HEAD_DIM)
    k_b = k_b.reshape(b, t, N_HEADS_B, HEAD_DIM)
    v_b = v_b.reshape(b, t, N_HEADS_B, HEAD_DIM)
    kv_a = jnp.stack([k_a, v_a], axis=2)
    kv_b = jnp.stack([k_b, v_b], axis=2)
    return q_a, kv_a, q_i, k_i, w_i, q_b, kv_b, g_a, g_b


def _index_scores(q_i, w_i, k_i):
    s = jnp.einsum('bthd,bsd->bths', q_i, k_i, preferred_element_type=F32)
    return jnp.einsum('bths,bth->bts', jax.nn.relu(s), w_i.astype(F32)) * (IDX_DIM ** -0.5 * N_IDX_HEADS ** -0.5)


def _attend_selected(q, k_sel, v_sel, valid):
    logits = jnp.einsum('bthd,btkhd->bthk', q, k_sel, preferred_element_type=F32) * HEAD_DIM ** -0.5
    logits = jnp.where(valid[:, :, None, :], logits, -jnp.inf)
    p = jax.nn.softmax(logits, axis=-1)
    return jnp.einsum('bthk,btkhd->bthd', p, v_sel.astype(F32)).astype(q.dtype)


def _dsa_prompt(q_a, kv_a, q_i, k_i, w_i):
    b, s = q_a.shape[:2]
    k_top = min(TOPK_MAX, s // 4)
    key_pos = jnp.arange(s)
    bidx = jnp.arange(b)[:, None, None]

    def block(args):
        q, qi, wi, i = args
        q_pos = i * Q_BLOCK + jnp.arange(Q_BLOCK)
        sc = _index_scores(qi, wi, k_i)
        sc = jnp.where(key_pos[None, None, :] <= q_pos[None, :, None], sc, -jnp.inf)
        _, idx = lax.top_k(sc, k_top)
        valid = idx <= q_pos[None, :, None]
        kv = kv_a[bidx, idx]
        return _attend_selected(q, kv[:, :, :, 0], kv[:, :, :, 1], valid)

    out = lax.map(block, (_to_blocks(q_a), _to_blocks(q_i), _to_blocks(w_i), jnp.arange(s // Q_BLOCK)))
    return _from_blocks(out)


def _dsa_sample(q_a, kv_a_new, q_i, k_i_new, w_i, cache_a_kv, cache_a_idx, page_table, layer):
    db, t = q_a.shape[:2]
    n_keys = PAST_LEN + t
    k_top = min(TOPK_MAX, n_keys // 4)
    k_i_past = cache_a_idx[layer, page_table].reshape(db, PAST_LEN, IDX_DIM)
    k_i_all = jnp.concatenate([k_i_past.astype(k_i_new.dtype), k_i_new], axis=1)
    sc = _index_scores(q_i, w_i, k_i_all)
    q_pos = PAST_LEN + jnp.arange(t)
    key_pos = jnp.arange(n_keys)
    sc = jnp.where(key_pos[None, None, :] <= q_pos[None, :, None], sc, -jnp.inf)
    _, idx = lax.top_k(sc, k_top)
    valid = idx <= q_pos[None, :, None]
    bidx = jnp.arange(db)[:, None, None]
    past_i = jnp.minimum(idx, PAST_LEN - 1)
    phys = page_table[bidx, past_i // PAGE_SIZE]
    kv_past = cache_a_kv[layer, phys, past_i % PAGE_SIZE]
    kv_cur = kv_a_new[bidx, jnp.clip(idx - PAST_LEN, 0, t - 1)]
    kv = jnp.where((idx < PAST_LEN)[..., None, None, None], kv_past.astype(kv_cur.dtype), kv_cur)
    return _attend_selected(q_a, kv[:, :, :, 0], kv[:, :, :, 1], valid)


def _sb_weights(z, mask, carry):
    log_beta = jax.nn.log_sigmoid(z)
    log_keep = jnp.where(mask, jax.nn.log_sigmoid(-z), 0.0)
    suffix = lax.cumsum(log_keep, axis=z.ndim - 1, reverse=True) - log_keep + carry[..., None]
    w = jnp.where(mask, jnp.exp(log_beta + suffix), 0.0)
    return w, carry + log_keep.sum(-1)


def _sb_prompt(q_b, kv_b):
    b, s = q_b.shape[:2]
    k, v = kv_b[:, :, 0], kv_b[:, :, 1].astype(F32)
    key_pos = jnp.arange(s)

    def block(args):
        q, i = args
        q_pos = i * Q_BLOCK + jnp.arange(Q_BLOCK)
        z = jnp.einsum('bthd,bshd->bhts', q, k, preferred_element_type=F32) * HEAD_DIM ** -0.5
        mask = key_pos[None, :] < q_pos[:, None]
        w, _ = _sb_weights(z, mask, jnp.zeros(z.shape[:-1], F32))
        return jnp.einsum('bhts,bshd->bthd', w, v)

    out = lax.map(block, (_to_blocks(q_b), jnp.arange(s // Q_BLOCK)))
    return _from_blocks(out).astype(q_b.dtype)


def _sb_sample(q_b, kv_b_new, cache_b_kv, page_table, layer):
    t = q_b.shape[1]
    scale = HEAD_DIM ** -0.5
    z = jnp.einsum('bthd,bshd->bhts', q_b, kv_b_new[:, :, 0], preferred_element_type=F32) * scale
    mask = jnp.arange(t)[None, :] < jnp.arange(t)[:, None]
    w, carry = _sb_weights(z, mask, jnp.zeros(z.shape[:-1], F32))
    acc = jnp.einsum('bhts,bshd->bthd', w, kv_b_new[:, :, 1].astype(F32))

    def step(state, phys):
        carry, acc = state
        kv = cache_b_kv[layer, phys]
        z = jnp.einsum('bthd,bshd->bhts', q_b, kv[:, :, 0], preferred_element_type=F32) * scale
        w, carry = _sb_weights(z, True, carry)
        acc = acc + jnp.einsum('bhts,bshd->bthd', w, kv[:, :, 1].astype(F32))
        return (carry, acc), None

    (_, acc), _ = lax.scan(step, (carry, acc), page_table.T[::-1])
    return acc.astype(q_b.dtype)


def _route(h, w_router, b_router):
    n = h.shape[0]
    scores = jax.nn.sigmoid(jnp.dot(h, w_router, preferred_element_type=F32))
    biased = scores + b_router.astype(F32)
    grp = biased.reshape(n, N_EXPERT_GROUPS, N_EXPERTS // N_EXPERT_GROUPS)
    grp_score = lax.top_k(grp, 2)[0].sum(-1)
    _, gidx = lax.top_k(grp_score, TOPK_GROUPS)
    gmask = jax.nn.one_hot(gidx, N_EXPERT_GROUPS, dtype=F32).sum(1) > 0
    masked = jnp.where(gmask[:, :, None], grp, -jnp.inf).reshape(n, N_EXPERTS)
    _, eidx = lax.top_k(masked, EXPERT_TOP_K)
    gw = jnp.take_along_axis(scores, eidx, axis=-1)
    gw = gw / gw.sum(-1, keepdims=True) * ROUTED_SCALE
    return eidx, gw


def _moe_block_rows(n_assign):
    br = MOE_MAX_BLOCK_ROWS
    while br > MOE_MIN_BLOCK_ROWS and n_assign < N_EXPERTS * br:
        br //= 2
    return br


def _routed_experts(h, eidx, gw, w_exp_gu, w_exp_down):
    n, d = h.shape
    n_assign = n * EXPERT_TOP_K
    br = _moe_block_rows(n_assign)
    n_blocks = -(-(n_assign + N_EXPERTS * (br - 1)) // br)
    rows = n_blocks * br
    flat_e = eidx.reshape(-1)
    order = jnp.argsort(flat_e)
    sorted_e = flat_e[order]
    tok = (order // EXPERT_TOP_K).astype(jnp.int32)
    w_sorted = gw.reshape(-1)[order]
    counts = jnp.bincount(flat_e, length=N_EXPERTS)
    padded = (counts + br - 1) // br * br
    start = jnp.cumsum(counts) - counts
    pad_end = jnp.cumsum(padded)
    pad_start = pad_end - padded
    dest = pad_start[sorted_e] + jnp.arange(n_assign) - start[sorted_e]
    tok_pad = jnp.full((rows,), n, jnp.int32).at[dest].set(tok)
    w_pad = jnp.zeros((rows,), F32).at[dest].set(w_sorted)
    block_expert = jnp.minimum(jnp.searchsorted(pad_end, jnp.arange(n_blocks) * br, side='right'), N_EXPERTS - 1)
    x_pad = jnp.concatenate([h, jnp.zeros((1, d), h.dtype)], axis=0)

    def step(acc, xs):
        t_blk, w_blk, e = xs
        xb = x_pad[t_blk]
        gate, up = jnp.split(jnp.dot(xb, w_exp_gu[e], preferred_element_type=F32), 2, axis=-1)
        act = (jax.nn.silu(gate) * up).astype(xb.dtype)
        y = jnp.dot(act, w_exp_down[e], preferred_element_type=F32) * w_blk[:, None]
        return acc.at[t_blk].add(y), None

    acc, _ = lax.scan(step, jnp.zeros((n + 1, d), F32),
                      (tok_pad.reshape(n_blocks, br), w_pad.reshape(n_blocks, br), block_expert))
    return acc[:n]


def _moe(h, w_router, b_router, w_exp_gu, w_exp_down, w_sh_gu, w_sh_down):
    eidx, gw = _route(h, w_router, b_router)
    routed = _routed_experts(h, eidx, gw, w_exp_gu, w_exp_down)
    g, u = jnp.split(h @ w_sh_gu, 2, axis=-1)
    shared = (jax.nn.silu(g) * u) @ w_sh_down
    return shared + routed.astype(h.dtype)


def _prompt_mixer(q_a, kv_a, q_i, k_i, w_i, q_b, kv_b):
    return _dsa_prompt(q_a, kv_a, q_i, k_i, w_i), _sb_prompt(q_b, kv_b)


def _sample_mixer(q_a, kv_a, q_i, k_i, w_i, q_b, kv_b, cache_a_kv, cache_a_idx, cache_b_kv, page_table, layer):
    o_a = _dsa_sample(q_a, kv_a, q_i, k_i, w_i, cache_a_kv, cache_a_idx, page_table, layer)
    o_b = _sb_sample(q_b, kv_b, cache_b_kv, page_table, layer)
    return o_a, o_b


def _layer(x, c, pos, mixer, w_ada, b_ada, g_pre_mix, g_post_mix, g_pre_ffn, g_post_ffn, w_in, w_proj_a,
           w_proj_b, w_out, w_router, b_router, w_exp_gu, w_exp_down, w_sh_gu, w_sh_down):
    b, t, d = x.shape
    mod = (jax.nn.silu(c) @ w_ada + b_ada)[:, None, :]
    shift1, scale1, gate1, shift2, scale2, gate2 = jnp.split(mod, 6, axis=-1)
    h = _rms_norm(x, g_pre_mix) * (1 + scale1) + shift1
    q_a, kv_a, q_i, k_i, w_i, q_b, kv_b, g_a, g_b = _project(h, w_in, pos)
    o_a, o_b = mixer(q_a, kv_a, q_i, k_i, w_i, q_b, kv_b)
    y_a = o_a.reshape(b, t, WIDTH_A) @ w_proj_a
    y_b = o_b.reshape(b, t, WIDTH_B) @ w_proj_b
    mix = (jax.nn.sigmoid(g_a) * y_a + jax.nn.sigmoid(g_b) * y_b) @ w_out
    x = x + gate1 * _rms_norm(mix, g_post_mix)
    h = _rms_norm(x, g_pre_ffn) * (1 + scale2) + shift2
    f = _moe(h.reshape(b * t, d), w_router, b_router, w_exp_gu, w_exp_down, w_sh_gu, w_sh_down).reshape(b, t, d)
    x = x + gate2 * _rms_norm(f, g_post_ffn)
    return x, (kv_a, k_i, kv_b)


def setup_inputs(seed: int = 0) -> dict:
    key = jax.random.key(seed)
    ks = jax.random.split(key, 24)
    n_pages = PAST_LEN // PAGE_SIZE
    n_used = DEC_BATCH * n_pages
    n_phys = (5 * n_used + 3) // 4
    D, L = D_MODEL, DEPTH

    def nrm(k, shape, scale=1.0):
        return jax.random.normal(k, shape, jnp.float32) * scale

    gains = 1.0 + nrm(ks[10], (4, L, D), 0.05)
    return {
        'x_prompt': nrm(ks[0], (BATCH, SEQ, D)),
        'x_sample': nrm(ks[1], (DEC_BATCH, DEC_SEQ, D)),
        'c_prompt': nrm(ks[2], (BATCH, D)),
        'c_sample': nrm(ks[3], (DEC_BATCH, D)),
        'cache_a_kv': nrm(ks[4], (L, n_phys, PAGE_SIZE, 2, N_HEADS_A, HEAD_DIM)),
        'cache_a_idx': nrm(ks[5], (L, n_phys, PAGE_SIZE, IDX_DIM)),
        'cache_b_kv': nrm(ks[6], (L, n_phys, PAGE_SIZE, 2, N_HEADS_B, HEAD_DIM)),
        'page_table': jax.random.permutation(ks[7], n_phys)[:n_used].reshape(DEC_BATCH, n_pages).astype(jnp.int32),
        'w_ada': nrm(ks[8], (L, D, 6 * D), 0.5 * D ** -0.5),
        'b_ada': nrm(ks[9], (L, 6 * D), 0.02),
        'g_pre_mix': gains[0],
        'g_post_mix': gains[1],
        'g_pre_ffn': gains[2],
        'g_post_ffn': gains[3],
        'w_in': nrm(ks[11], (L, D, IN_COLS), D ** -0.5),
        'w_proj_a': nrm(ks[12], (L, WIDTH_A, D), WIDTH_A ** -0.5),
        'w_proj_b': nrm(ks[13], (L, WIDTH_B, D), WIDTH_B ** -0.5),
        'w_out': nrm(ks[14], (L, D, D), D ** -0.5),
        'w_router': nrm(ks[15], (L, D, N_EXPERTS), D ** -0.5),
        'b_router': nrm(ks[16], (L, N_EXPERTS), 0.01),
        'w_exp_gu': nrm(ks[17], (L, N_EXPERTS, D, 2 * D_EXPERT), D ** -0.5),
        'w_exp_down': nrm(ks[18], (L, N_EXPERTS, D_EXPERT, D), D_EXPERT ** -0.5),
        'w_sh_gu': nrm(ks[19], (L, D, 2 * D_SHARED), D ** -0.5),
        'w_sh_down': nrm(ks[20], (L, D_SHARED, D), D_SHARED ** -0.5),
    }


def reference(x_prompt, x_sample, c_prompt, c_sample, cache_a_kv, cache_a_idx, cache_b_kv, page_table,
              w_ada, b_ada, g_pre_mix, g_post_mix, g_pre_ffn, g_post_ffn, w_in, w_proj_a, w_proj_b, w_out,
              w_router, b_router, w_exp_gu, w_exp_down, w_sh_gu, w_sh_down):
    pos_prompt = jnp.arange(x_prompt.shape[1], dtype=jnp.int32)
    pos_sample = PAST_LEN + jnp.arange(x_sample.shape[1], dtype=jnp.int32)
    y_prompt, y_sample = x_prompt, x_sample
    new_p, new_s = [], []
    for l in range(DEPTH):
        lw = (w_ada[l], b_ada[l], g_pre_mix[l], g_post_mix[l], g_pre_ffn[l], g_post_ffn[l], w_in[l],
              w_proj_a[l], w_proj_b[l], w_out[l], w_router[l], b_router[l], w_exp_gu[l], w_exp_down[l],
              w_sh_gu[l], w_sh_down[l])
        y_prompt, st_p = _layer(y_prompt, c_prompt, pos_prompt, _prompt_mixer, *lw)
        sample_mixer = functools.partial(_sample_mixer, cache_a_kv=cache_a_kv, cache_a_idx=cache_a_idx,
                                         cache_b_kv=cache_b_kv, page_table=page_table, layer=l)
        y_sample, st_s = _layer(y_sample, c_sample, pos_sample, sample_mixer, *lw)
        new_p.append(st_p)
        new_s.append(st_s)
    a_kv_prompt = jnp.stack([s[0] for s in new_p])
    a_idx_prompt = jnp.stack([s[1] for s in new_p])
    b_kv_prompt = jnp.stack([s[2] for s in new_p])
    a_kv_sample = jnp.stack([s[0] for s in new_s])
    a_idx_sample = jnp.stack([s[1] for s in new_s])
    b_kv_sample = jnp.stack([s[2] for s in new_s])
    return (y_prompt, y_sample, a_kv_prompt, a_idx_prompt, b_kv_prompt, a_kv_sample, a_idx_sample, b_kv_sample)
```

```python
import functools

import jax
import jax.numpy as jnp
from jax import lax
from jax.experimental import pallas as pl
from jax.experimental.pallas import tpu as pltpu

F32 = jnp.float32
BF16 = jnp.bfloat16
I32 = jnp.int32

HEAD_DIM = 64
N_HEADS = 8
WIDTH = N_HEADS * HEAD_DIM
IDX_DIM = 64
N_IDX_HEADS = 8
TOPK_MAX = 256
ROPE_THETA = 10000.0
RMS_EPS = 1e-6
EXPERT_TOP_K = 8
N_EXPERT_GROUPS = 8
TOPK_GROUPS = 4
ROUTED_SCALE = 2.5

LANES = 128
ROW_TILE = 256
ATT_TILE = 256
PAGES_PER_STEP = 8
MOE_BLOCK_ROWS = 128
VMEM_LIMIT = 56 * 1024 * 1024
INT_MIN = -(2 ** 31)
NEG_INF = float("-inf")


def _cparams(*sem):
    return pltpu.CompilerParams(dimension_semantics=sem, vmem_limit_bytes=VMEM_LIMIT)


def _dot(a, b):
    return jnp.dot(a, b, preferred_element_type=F32)


def _dot_nt(a, b):
    return lax.dot_general(a, b, (((1,), (1,)), ((), ())), preferred_element_type=F32)


def _split_bf16(v):
    hi = v.astype(BF16)
    lo = (v - hi.astype(F32)).astype(BF16)
    return hi, lo


def _rms(v, g):
    return v * lax.rsqrt(jnp.mean(v * v, axis=-1, keepdims=True) + RMS_EPS) * g


def _ada_kernel(c_ref, w_ref, b_ref, o_ref):
    c = c_ref[...]
    a_hi, a_lo = _split_bf16(c * jax.nn.sigmoid(c))
    w_hi, w_lo = _split_bf16(w_ref[...])
    o_ref[...] = _dot(a_hi, w_hi) + _dot(a_hi, w_lo) + _dot(a_lo, w_hi) + b_ref[...]


def _ada(c, w, b):
    m, d = c.shape
    n = w.shape[1]
    tn = 1024 if n % 1024 == 0 else n
    return pl.pallas_call(
        _ada_kernel,
        grid=(n // tn,),
        in_specs=[pl.BlockSpec((m, d), lambda j: (0, 0)),
                  pl.BlockSpec((d, tn), lambda j: (0, j)),
                  pl.BlockSpec((1, tn), lambda j: (0, j))],
        out_specs=pl.BlockSpec((m, tn), lambda j: (0, j)),
        out_shape=jax.ShapeDtypeStruct((m, n), F32),
        compiler_params=_cparams("arbitrary"),
        name="ada_mod",
    )(c, w, b.reshape(1, n))


def _rope_cols(v, cos, sin_signed):
    rows = v.shape[0]
    lane = lax.broadcasted_iota(I32, (rows, LANES), 1)
    first_half = (lane % HEAD_DIM) < (HEAD_DIM // 2)
    outs = []
    for c in range(v.shape[1] // LANES):
        ch = v[:, c * LANES:(c + 1) * LANES]
        partner = jnp.where(first_half, pltpu.roll(ch, LANES - HEAD_DIM // 2, 1), pltpu.roll(ch, HEAD_DIM // 2, 1))
        outs.append(ch * cos + partner * sin_signed)
    return outs[0] if len(outs) == 1 else jnp.concatenate(outs, axis=1)


def _inproj_kernel(x_ref, sc_ref, sh_ref, g_ref, cos_ref, sin_ref,
                   wqa_ref, wkva_ref, wqi_ref, wsm_ref, wqb_ref, wkvb_ref, wg_ref,
                   qa_o, qi_o, qb_o, kva_o, kvb_o, sm_o, ka_o, kb_o, ki_o, ga_o, gb_o, *vt_o):
    d = x_ref.shape[1]
    h = _rms(x_ref[...], g_ref[...]) * (1.0 + sc_ref[0]) + sh_ref[0]
    hb = h.astype(BF16)
    cos = cos_ref[...]
    sin = sin_ref[...]

    qa_o[...] = _rope_cols(_dot(hb, wqa_ref[...]), cos, sin).astype(BF16)
    qi_o[...] = _rope_cols(_dot(hb, wqi_ref[...]), cos, sin).astype(BF16)
    qb_o[...] = _dot(hb, wqb_ref[...]).astype(BF16)

    kva = _dot(hb, wkva_ref[...])
    ka = _rope_cols(kva[:, :WIDTH], cos, sin)
    va = kva[:, WIDTH:]
    kva_o[:, :WIDTH] = ka
    kva_o[:, WIDTH:] = va
    ka_o[...] = ka.astype(BF16)

    kvb = _dot(hb, wkvb_ref[...])
    kvb_o[...] = kvb
    kb_o[...] = kvb[:, :WIDTH].astype(BF16)

    sm = _dot(hb, wsm_ref[...])
    lane = lax.broadcasted_iota(I32, sm.shape, 1)
    sm = jnp.where(lane < IDX_DIM, _rope_cols(sm, cos, sin), sm)
    sm_o[...] = sm
    ki_o[...] = sm.astype(BF16)

    g = jax.nn.sigmoid(_dot(hb, wg_ref[...]))
    ga_o[...] = g[:, :d]
    gb_o[...] = g[:, d:]

    if vt_o:
        vat_o, vbt_o = vt_o
        vat_o[0, 0] = va.T.astype(BF16)
        vbt_o[0, 0] = kvb[:, WIDTH:].T.astype(BF16)


def _inproj(x, scale, shift, mod_map, g, cos, sin, tbl_blocks, ws, *, tm, seq_tiles):
    n, d = x.shape
    r = scale.shape[1]
    grid = (n // tm,)
    row = lambda i: (i, 0)
    const = lambda i: (0, 0)
    in_specs = [pl.BlockSpec((tm, d), row),
                pl.BlockSpec((1, r, d), mod_map),
                pl.BlockSpec((1, r, d), mod_map),
                pl.BlockSpec((1, d), const),
                pl.BlockSpec((tm, LANES), lambda i: (i % tbl_blocks, 0)),
                pl.BlockSpec((tm, LANES), lambda i: (i % tbl_blocks, 0))]
    in_specs += [pl.BlockSpec(w.shape, const) for w in ws]
    out_cols = [(WIDTH, BF16), (WIDTH, BF16), (WIDTH, BF16), (2 * WIDTH, F32), (2 * WIDTH, F32), (LANES, F32),
                (WIDTH, BF16), (WIDTH, BF16), (LANES, BF16), (d, F32), (d, F32)]
    out_specs = [pl.BlockSpec((tm, c), row) for c, _ in out_cols]
    out_shape = [jax.ShapeDtypeStruct((n, c), t) for c, t in out_cols]
    if seq_tiles is not None:
        nb = n // (tm * seq_tiles)
        vt_map = lambda i: (i // seq_tiles, i % seq_tiles, 0, 0)
        out_specs += [pl.BlockSpec((1, 1, WIDTH, tm), vt_map)] * 2
        out_shape += [jax.ShapeDtypeStruct((nb, seq_tiles, WIDTH, tm), BF16)] * 2
    return pl.pallas_call(
        _inproj_kernel, grid=grid, in_specs=in_specs, out_specs=out_specs, out_shape=out_shape,
        compiler_params=_cparams("arbitrary"), name="in_proj",
    )(x, scale, shift, g.reshape(1, d), cos, sin, *ws)


def _sortable(v):
    b = lax.bitcast_convert_type(v, I32)
    return b ^ ((b >> 31) & jnp.int32(0x7FFFFFFF))


def _select_topk(load_keys, n_chunks, part_sum, final_sum, pos_of_chunk, shape1, part_shape, k_top, idx_bits):
    def count(pred):
        def body(c, acc):
            return acc + part_sum(jnp.where(pred(load_keys(c), c), 1.0, 0.0))
        return final_sum(lax.fori_loop(0, n_chunks, body, jnp.zeros(part_shape, F32)))

    def bit_body(bi, t_u):
        cand_u = t_u | jnp.left_shift(jnp.int32(1), 31 - bi)
        cand_s = cand_u ^ jnp.int32(INT_MIN)
        cnt = count(lambda k, c: k >= cand_s)
        return jnp.where(cnt >= k_top, cand_u, t_u)

    t_u = lax.fori_loop(0, 32, bit_body, jnp.zeros(shape1, I32))
    thr = t_u ^ jnp.int32(INT_MIN)
    need = k_top - count(lambda k, c: k > thr)

    def idx_body(bi, cut):
        cand = cut | jnp.left_shift(jnp.int32(1), idx_bits - 1 - bi)
        below = count(lambda k, c: (k == thr) & (pos_of_chunk(c) < cand))
        return jnp.where(below < need, cand, cut)

    cut = lax.fori_loop(0, idx_bits, idx_body, jnp.zeros(shape1, I32))
    return thr, cut


def _dsa_prompt_kernel(qi_ref, qa_ref, sm_ref, ki_ref, ka_ref, vat_ref, o_ref, keys_scr, acc_scr, m_scr, l_scr,
                       *, k_top, idx_bits):
    t = qi_ref.shape[0]
    i = pl.program_id(1)
    n_chunks = i + 1
    q_pos = i * t + lax.broadcasted_iota(I32, (t, t), 1)
    k_iota = lax.broadcasted_iota(I32, (t, t), 0)

    qit = qi_ref[...].astype(F32).T.astype(BF16)
    smt = sm_ref[...].T
    idx_scale = IDX_DIM ** -0.5 * N_IDX_HEADS ** -0.5

    def score_chunk(c, carry):
        off = pl.multiple_of(c * t, t)
        kc = ki_ref[pl.ds(off, t), :][:, :IDX_DIM]
        acc = jnp.zeros((t, t), F32)
        for h in range(N_IDX_HEADS):
            s = _dot(kc, qit[h * IDX_DIM:(h + 1) * IDX_DIM, :])
            acc = acc + jnp.maximum(s, 0.0) * smt[IDX_DIM + h:IDX_DIM + h + 1, :]
        key = _sortable(acc * idx_scale)
        keys_scr[pl.ds(off, t), :] = jnp.where(c * t + k_iota <= q_pos, key, jnp.int32(INT_MIN))
        return carry

    lax.fori_loop(0, n_chunks, score_chunk, 0)

    def load_keys(c):
        return keys_scr[pl.ds(pl.multiple_of(c * t, t), t), :]

    thr, cut = _select_topk(load_keys, n_chunks, lambda x: x.reshape(t // 8, 8, t).sum(axis=0),
                            lambda p: jnp.sum(p, axis=0, keepdims=True), lambda c: c * t + k_iota,
                            (1, t), (8, t), k_top, idx_bits)

    qat = (qa_ref[...].astype(F32) * HEAD_DIM ** -0.5).T.astype(BF16)
    acc_scr[...] = jnp.zeros_like(acc_scr)
    m_scr[...] = jnp.full_like(m_scr, NEG_INF)
    l_scr[...] = jnp.zeros_like(l_scr)

    def att_chunk(c, carry):
        off = pl.multiple_of(c * t, t)
        key = keys_scr[pl.ds(off, t), :]
        sel = ((key > thr) | ((key == thr) & (c * t + k_iota <= cut))) & (key != jnp.int32(INT_MIN))
        for h in range(N_HEADS):
            hs = slice(h * HEAD_DIM, (h + 1) * HEAD_DIM)
            lt = _dot(ka_ref[pl.ds(off, t), hs], qat[hs, :])
            lt = jnp.where(sel, lt, NEG_INF)
            m_old = m_scr[h:h + 1, :]
            m_new = jnp.maximum(m_old, jnp.max(lt, axis=0, keepdims=True))
            m_safe = jnp.where(m_new == NEG_INF, 0.0, m_new)
            alpha = jnp.exp(m_old - m_safe)
            p = jnp.exp(lt - m_safe)
            l_scr[h:h + 1, :] = alpha * l_scr[h:h + 1, :] + jnp.sum(p, axis=0, keepdims=True)
            m_scr[h:h + 1, :] = m_new
            acc_scr[hs, :] = alpha * acc_scr[hs, :] + _dot(vat_ref[c, hs, :], p.astype(BF16))
        return carry

    lax.fori_loop(0, n_chunks, att_chunk, 0)

    for h in range(N_HEADS):
        hs = slice(h * HEAD_DIM, (h + 1) * HEAD_DIM)
        acc_scr[hs, :] = acc_scr[hs, :] * (1.0 / l_scr[h:h + 1, :])
    o_ref[...] = acc_scr[...].T.astype(BF16)


def _dsa_prompt(qi, qa, sm, ki, ka, vat, *, b, s, k_top):
    t = ATT_TILE
    nq = s // t
    qmap = lambda bi, i: (bi * nq + i, 0)
    kmap = lambda bi, i: (bi, 0)
    return pl.pallas_call(
        functools.partial(_dsa_prompt_kernel, k_top=k_top, idx_bits=max(1, (s - 1).bit_length())),
        grid=(b, nq),
        in_specs=[pl.BlockSpec((t, WIDTH), qmap), pl.BlockSpec((t, WIDTH), qmap), pl.BlockSpec((t, LANES), qmap),
                  pl.BlockSpec((s, LANES), kmap), pl.BlockSpec((s, WIDTH), kmap),
                  pl.BlockSpec((None, nq, WIDTH, t), lambda bi, i: (bi, 0, 0, 0))],
        out_specs=pl.BlockSpec((t, WIDTH), qmap),
        out_shape=jax.ShapeDtypeStruct((b * s, WIDTH), BF16),
        scratch_shapes=[pltpu.VMEM((s, t), I32), pltpu.VMEM((WIDTH, t), F32),
                        pltpu.VMEM((N_HEADS, t), F32), pltpu.VMEM((N_HEADS, t), F32)],
        compiler_params=_cparams("arbitrary", "arbitrary"),
        name="dsa_prompt",
    )(qi, qa, sm, ki, ka, vat)


def _log_sigmoid_pair(z):
    tail = jnp.log1p(jnp.exp(-jnp.abs(z)))
    return jnp.minimum(z, 0.0) - tail, jnp.minimum(-z, 0.0) - tail


def _sb_prompt_kernel(qb_ref, kb_ref, vbt_ref, o_ref, acc_scr, carry_scr):
    t = qb_ref.shape[0]
    i = pl.program_id(1)
    q_pos = i * t + lax.broadcasted_iota(I32, (t, t), 1)
    k_iota = lax.broadcasted_iota(I32, (t, t), 0)
    later = (lax.broadcasted_iota(I32, (t, t), 1) > k_iota).astype(BF16)
    qt = (qb_ref[...].astype(F32) * HEAD_DIM ** -0.5).T.astype(BF16)
    acc_scr[...] = jnp.zeros_like(acc_scr)
    carry_scr[...] = jnp.zeros_like(carry_scr)

    def chunk(j, carry):
        c = i - j
        off = pl.multiple_of(c * t, t)
        mask = c * t + k_iota < q_pos
        for h in range(N_HEADS):
            hs = slice(h * HEAD_DIM, (h + 1) * HEAD_DIM)
            zt = _dot(kb_ref[pl.ds(off, t), hs], qt[hs, :])
            log_beta, log_keep = _log_sigmoid_pair(zt)
            log_keep = jnp.where(mask, log_keep, 0.0)
            keep_hi, keep_lo = _split_bf16(log_keep)
            suffix = _dot(later, keep_hi) + _dot(later, keep_lo) + carry_scr[h:h + 1, :]
            w = jnp.where(mask, jnp.exp(log_beta + suffix), 0.0)
            acc_scr[hs, :] = acc_scr[hs, :] + _dot(vbt_ref[c, hs, :], w.astype(BF16))
            carry_scr[h:h + 1, :] = carry_scr[h:h + 1, :] + jnp.sum(log_keep, axis=0, keepdims=True)
        return carry

    lax.fori_loop(0, i + 1, chunk, 0)
    o_ref[...] = acc_scr[...].T.astype(BF16)


def _sb_prompt(qb, kb, vbt, *, b, s):
    t = ATT_TILE
    nq = s // t
    qmap = lambda bi, i: (bi * nq + i, 0)
    return pl.pallas_call(
        _sb_prompt_kernel,
        grid=(b, nq),
        in_specs=[pl.BlockSpec((t, WIDTH), qmap), pl.BlockSpec((s, WIDTH), lambda bi, i: (bi, 0)),
                  pl.BlockSpec((None, nq, WIDTH, t), lambda bi, i: (bi, 0, 0, 0))],
        out_specs=pl.BlockSpec((t, WIDTH), qmap),
        out_shape=jax.ShapeDtypeStruct((b * s, WIDTH), BF16),
        scratch_shapes=[pltpu.VMEM((WIDTH, t), F32), pltpu.VMEM((N_HEADS, t), F32)],
        compiler_params=_cparams("arbitrary", "arbitrary"),
        name="sb_prompt",
    )(qb, kb, vbt)


def _rows_by_head(q):
    return jnp.concatenate([q[:, h * HEAD_DIM:(h + 1) * HEAD_DIM] for h in range(N_HEADS)], axis=0)


def _block_diag_rows(q):
    t = q.shape[0]
    rows = jnp.concatenate([q] * N_HEADS, axis=0)
    r = lax.broadcasted_iota(I32, rows.shape, 0) // t
    c = lax.broadcasted_iota(I32, rows.shape, 1) // HEAD_DIM
    return jnp.where(r == c, rows, 0.0)


def _diag_heads(acc, t):
    return jnp.concatenate([acc[h * t:(h + 1) * t, h * HEAD_DIM:(h + 1) * HEAD_DIM] for h in range(N_HEADS)], axis=1)


def _pad_rows(v, rows):
    return jnp.concatenate([v, jnp.zeros((rows - v.shape[0], v.shape[1]), v.dtype)], axis=0)


def _dsa_sample_select_kernel(pt_ref, *refs, k_top, idx_bits, n_steps):
    pages = refs[:PAGES_PER_STEP]
    qi_ref, sm_ref, bias_o, keys_scr = refs[PAGES_PER_STEP:]
    j = pl.program_id(1)
    t = qi_ref.shape[0]
    page = pages[0].shape[0]
    cw = PAGES_PER_STEP * page
    idx_scale = IDX_DIM ** -0.5 * N_IDX_HEADS ** -0.5
    q_rows = _rows_by_head(qi_ref[...].astype(F32)).astype(BF16)
    sm = sm_ref[...]
    w_cols = [jnp.broadcast_to(sm[:, IDX_DIM + h:IDX_DIM + h + 1], (t, page)) for h in range(N_IDX_HEADS)]

    def scores(k_bf):
        s = _dot_nt(q_rows, k_bf)
        acc = jnp.zeros((t, page), F32)
        for h in range(N_IDX_HEADS):
            acc = acc + jnp.maximum(s[h * t:(h + 1) * t, :], 0.0) * w_cols[h]
        return acc * idx_scale

    for r in range(PAGES_PER_STEP):
        keys_scr[j, :, r * page:(r + 1) * page] = _sortable(scores(pages[r][...].astype(BF16)))

    @pl.when(j == n_steps - 1)
    def _():
        k_new = _pad_rows(sm[:, :IDX_DIM], page).astype(BF16)
        key = _sortable(scores(k_new))
        u = lax.broadcasted_iota(I32, (t, page), 1)
        q = lax.broadcasted_iota(I32, (t, page), 0)
        keys_scr[n_steps, :, :page] = jnp.where(u <= q, key, jnp.int32(INT_MIN))
        keys_scr[n_steps, :, page:] = jnp.full((t, cw - page), INT_MIN, I32)

        lane = lax.broadcasted_iota(I32, (t, cw), 1)
        part = lambda x: functools.reduce(lambda a, b2: a + b2, [x[:, k * page:(k + 1) * page] for k in range(PAGES_PER_STEP)])
        final = lambda p: jnp.sum(p, axis=1, keepdims=True)
        thr, cut = _select_topk(lambda c: keys_scr[c], n_steps + 1, part, final, lambda c: c * cw + lane,
                                (t, 1), (t, page), k_top, idx_bits)

        def write(c, carry):
            key = keys_scr[c]
            sel = ((key > thr) | ((key == thr) & (c * cw + lane <= cut))) & (key != jnp.int32(INT_MIN))
            bias_o[c] = jnp.where(sel, 0.0, NEG_INF)
            return carry

        lax.fori_loop(0, n_steps + 1, write, 0)


def _dsa_sample_attend_kernel(pt_ref, *refs, n_steps):
    pages = refs[:PAGES_PER_STEP]
    qa_ref, kv_new_ref, bias_ref, bias_new_ref, o_ref, acc_scr, m_scr, l_scr = refs[PAGES_PER_STEP:]
    j = pl.program_id(1)
    t = qa_ref.shape[0]
    page = pages[0].shape[0]
    q_bd = _block_diag_rows(qa_ref[...].astype(F32) * HEAD_DIM ** -0.5).astype(BF16)

    @pl.when(j == 0)
    def _():
        acc_scr[...] = jnp.zeros_like(acc_scr)
        m_scr[...] = jnp.full_like(m_scr, NEG_INF)
        l_scr[...] = jnp.zeros_like(l_scr)

    def update(k_bf, v_bf, bias):
        logits = _dot_nt(q_bd, k_bf) + jnp.concatenate([bias] * N_HEADS, axis=0)
        m_old = m_scr[...]
        m_new = jnp.maximum(m_old, jnp.max(logits, axis=1, keepdims=True))
        m_safe = jnp.where(m_new == NEG_INF, 0.0, m_new)
        alpha = jnp.exp(m_old - m_safe)
        p = jnp.exp(logits - m_safe)
        l_scr[...] = alpha * l_scr[...] + jnp.sum(p, axis=1, keepdims=True)
        m_scr[...] = m_new
        acc_scr[...] = alpha * acc_scr[...] + _dot(p.astype(BF16), v_bf)

    k_all = jnp.concatenate([pg[:, :WIDTH] for pg in pages], axis=0).astype(BF16)
    v_all = jnp.concatenate([pg[:, WIDTH:] for pg in pages], axis=0).astype(BF16)
    update(k_all, v_all, bias_ref[...])

    @pl.when(j == n_steps - 1)
    def _():
        kv_new = _pad_rows(kv_new_ref[...], page).astype(BF16)
        update(kv_new[:, :WIDTH], kv_new[:, WIDTH:], bias_new_ref[:, :page])
        o_ref[...] = _diag_heads(acc_scr[...] * (1.0 / l_scr[...]), t).astype(BF16)


def _page_specs(cache, layer, n_pages, reverse=False):
    page, width = cache.shape[2:]

    def spec(r):
        if reverse:
            return pl.BlockSpec((None, None, page, width),
                                lambda b, j, pt: (layer, pt[b, n_pages - 1 - (j * PAGES_PER_STEP + r)], 0, 0))
        return pl.BlockSpec((None, None, page, width), lambda b, j, pt: (layer, pt[b, j * PAGES_PER_STEP + r], 0, 0))
    return [spec(r) for r in range(PAGES_PER_STEP)]


def _dsa_sample(qi, qa, sm, kva_new, cache_idx, cache_kv, page_table, layer, *, k_top):
    db, t, _ = qi.shape
    n_pages = page_table.shape[1]
    n_steps = n_pages // PAGES_PER_STEP
    page = cache_idx.shape[2]
    cw = PAGES_PER_STEP * page
    per_b = lambda width: pl.BlockSpec((None, t, width), lambda b, j, pt: (b, 0, 0))
    n_keys = n_pages * page + t
    bias = pl.pallas_call(
        functools.partial(_dsa_sample_select_kernel, k_top=k_top, idx_bits=max(1, (n_keys - 1).bit_length()), n_steps=n_steps),
        grid_spec=pltpu.PrefetchScalarGridSpec(
            num_scalar_prefetch=1, grid=(db, n_steps),
            in_specs=_page_specs(cache_idx, layer, n_pages) + [per_b(WIDTH), per_b(LANES)],
            out_specs=pl.BlockSpec((None, n_steps + 1, t, cw), lambda b, j, pt: (b, 0, 0, 0)),
            scratch_shapes=[pltpu.VMEM((n_steps + 1, t, cw), I32)]),
        out_shape=jax.ShapeDtypeStruct((db, n_steps + 1, t, cw), F32),
        compiler_params=_cparams("arbitrary", "arbitrary"),
        name="dsa_sample_select",
    )(page_table, *([cache_idx] * PAGES_PER_STEP), qi, sm)
    return pl.pallas_call(
        functools.partial(_dsa_sample_attend_kernel, n_steps=n_steps),
        grid_spec=pltpu.PrefetchScalarGridSpec(
            num_scalar_prefetch=1, grid=(db, n_steps),
            in_specs=_page_specs(cache_kv, layer, n_pages) + [
                per_b(WIDTH), per_b(2 * WIDTH),
                pl.BlockSpec((None, None, t, cw), lambda b, j, pt: (b, j, 0, 0)),
                pl.BlockSpec((None, None, t, cw), lambda b, j, pt: (b, n_steps, 0, 0))],
            out_specs=per_b(WIDTH),
            scratch_shapes=[pltpu.VMEM((N_HEADS * t, WIDTH), F32), pltpu.VMEM((N_HEADS * t, 1), F32),
                            pltpu.VMEM((N_HEADS * t, 1), F32)]),
        out_shape=jax.ShapeDtypeStruct((db, t, WIDTH), BF16),
        compiler_params=_cparams("arbitrary", "arbitrary"),
        name="dsa_sample_attend",
    )(page_table, *([cache_kv] * PAGES_PER_STEP), qa, kva_new, bias, bias)


def _suffix_scan_lanes(x):
    lane = lax.broadcasted_iota(I32, x.shape, 1)
    sh = 1
    while sh < LANES:
        x = x + jnp.where(lane + sh < LANES, pltpu.roll(x, LANES - sh, 1), 0.0)
        sh *= 2
    return x


def _sb_sample_kernel(pt_ref, *refs, n_steps):
    pages = refs[:PAGES_PER_STEP]
    qb_ref, kv_new_ref, o_ref, acc_scr, carry_scr = refs[PAGES_PER_STEP:]
    j = pl.program_id(1)
    t = qb_ref.shape[0]
    page = pages[0].shape[0]
    q_bd = _block_diag_rows(qb_ref[...].astype(F32) * HEAD_DIM ** -0.5).astype(BF16)

    def weights(z, mask):
        outs = []
        carry = carry_scr[...]
        for k in range(z.shape[1] // page):
            zt = z[:, k * page:(k + 1) * page]
            log_beta, log_keep = _log_sigmoid_pair(zt)
            if mask is not None:
                log_keep = jnp.where(mask, log_keep, 0.0)
            incl = _suffix_scan_lanes(log_keep)
            w = jnp.exp(log_beta + (incl - log_keep) + carry)
            if mask is not None:
                w = jnp.where(mask, w, 0.0)
            outs.append(w)
            carry = carry + incl[:, :1]
        carry_scr[...] = carry
        return outs[0] if len(outs) == 1 else jnp.concatenate(outs, axis=1)

    @pl.when(j == 0)
    def _():
        carry_scr[...] = jnp.zeros_like(carry_scr)
        kv_new = _pad_rows(kv_new_ref[...], page).astype(BF16)
        u = lax.broadcasted_iota(I32, (N_HEADS * t, page), 1)
        q = lax.broadcasted_iota(I32, (N_HEADS * t, page), 0) % t
        w = weights(_dot_nt(q_bd, kv_new[:, :WIDTH]), u < q)
        acc_scr[...] = _dot(w.astype(BF16), kv_new[:, WIDTH:])

    k_all = jnp.concatenate([pg[:, :WIDTH] for pg in pages], axis=0).astype(BF16)
    v_all = jnp.concatenate([pg[:, WIDTH:] for pg in pages], axis=0).astype(BF16)
    w = weights(_dot_nt(q_bd, k_all), None)
    acc_scr[...] = acc_scr[...] + _dot(w.astype(BF16), v_all)

    @pl.when(j == n_steps - 1)
    def _():
        o_ref[...] = _diag_heads(acc_scr[...], t).astype(BF16)


def _sb_sample(qb, kvb_new, cache_kv, page_table, layer):
    db, t, _ = qb.shape
    n_pages = page_table.shape[1]
    n_steps = n_pages // PAGES_PER_STEP
    per_b = lambda width: pl.BlockSpec((None, t, width), lambda b, j, pt: (b, 0, 0))
    return pl.pallas_call(
        functools.partial(_sb_sample_kernel, n_steps=n_steps),
        grid_spec=pltpu.PrefetchScalarGridSpec(
            num_scalar_prefetch=1, grid=(db, n_steps),
            in_specs=_page_specs(cache_kv, layer, n_pages, reverse=True) + [per_b(WIDTH), per_b(2 * WIDTH)],
            out_specs=per_b(WIDTH),
            scratch_shapes=[pltpu.VMEM((N_HEADS * t, WIDTH), F32), pltpu.VMEM((N_HEADS * t, 1), F32)]),
        out_shape=jax.ShapeDtypeStruct((db, t, WIDTH), BF16),
        compiler_params=_cparams("arbitrary", "arbitrary"),
        name="sb_sample",
    )(page_table, *([cache_kv] * PAGES_PER_STEP), qb, kvb_new)


def _outproj_kernel(oa_ref, ob_ref, ga_ref, gb_ref, x_ref, gate_ref, sc_ref, sh_ref, gpost_ref, gpre_ref,
                    wpa_ref, wpb_ref, wout_ref, wrh_ref, wrl_ref, x1_o, h2_o, lg_o):
    mix = ga_ref[...] * _dot(oa_ref[...], wpa_ref[...]) + gb_ref[...] * _dot(ob_ref[...], wpb_ref[...])
    mo = _dot(mix.astype(BF16), wout_ref[...])
    x1 = x_ref[...] + gate_ref[0] * _rms(mo, gpost_ref[...])
    x1_o[...] = x1
    h2 = _rms(x1, gpre_ref[...]) * (1.0 + sc_ref[0]) + sh_ref[0]
    h_hi, h_lo = _split_bf16(h2)
    h2_o[...] = h_hi
    lg_o[...] = _dot_nt(wrh_ref[...], h_hi) + _dot_nt(wrh_ref[...], h_lo) + _dot_nt(wrl_ref[...], h_hi)


def _outproj(oa, ob, ga, gb, x, gate, scale, shift, mod_map, gpost, gpre, ws, *, tm):
    n, d = x.shape
    r = gate.shape[1]
    e = ws[3].shape[0]
    row = lambda i: (i, 0)
    const = lambda i: (0, 0)
    in_specs = [pl.BlockSpec((tm, WIDTH), row), pl.BlockSpec((tm, WIDTH), row),
                pl.BlockSpec((tm, d), row), pl.BlockSpec((tm, d), row), pl.BlockSpec((tm, d), row),
                pl.BlockSpec((1, r, d), mod_map), pl.BlockSpec((1, r, d), mod_map), pl.BlockSpec((1, r, d), mod_map),
                pl.BlockSpec((1, d), const), pl.BlockSpec((1, d), const)]
    in_specs += [pl.BlockSpec(w.shape, const) for w in ws]
    return pl.pallas_call(
        _outproj_kernel, grid=(n // tm,), in_specs=in_specs,
        out_specs=[pl.BlockSpec((tm, d), row), pl.BlockSpec((tm, d), row), pl.BlockSpec((e, tm), lambda i: (0, i))],
        out_shape=[jax.ShapeDtypeStruct((n, d), F32), jax.ShapeDtypeStruct((n, d), BF16),
                   jax.ShapeDtypeStruct((e, n), F32)],
        compiler_params=_cparams("arbitrary"), name="out_proj",
    )(oa, ob, ga, gb, x, gate, scale, shift, gpost.reshape(1, d), gpre.reshape(1, d), *ws)


def _route_kernel(lg_ref, b_ref, eidx_o, gw_o):
    scores = jax.nn.sigmoid(lg_ref[...])
    biased = scores + b_ref[...]
    e, tn = scores.shape
    gsz = e // N_EXPERT_GROUPS
    group_score = []
    for g in range(N_EXPERT_GROUPS):
        xg = biased[g * gsz:(g + 1) * gsz, :]
        m1 = jnp.max(xg, axis=0, keepdims=True)
        n1 = jnp.sum(jnp.where(xg == m1, 1.0, 0.0), axis=0, keepdims=True)
        m2 = jnp.max(jnp.where(xg < m1, xg, NEG_INF), axis=0, keepdims=True)
        group_score.append(m1 + jnp.where(n1 >= 2.0, m1, m2))
    cur = []
    for g in range(N_EXPERT_GROUPS):
        rank = jnp.zeros((1, tn), F32)
        for g2 in range(N_EXPERT_GROUPS):
            if g2 == g:
                continue
            ahead = (group_score[g2] > group_score[g]) | ((group_score[g2] == group_score[g]) & (g2 < g))
            rank = rank + jnp.where(ahead, 1.0, 0.0)
        cur.append(jnp.where(rank < TOPK_GROUPS, biased[g * gsz:(g + 1) * gsz, :], NEG_INF))
    cur = jnp.concatenate(cur, axis=0)
    row = lax.broadcasted_iota(I32, (e, tn), 0).astype(F32)
    picked = []
    for k in range(EXPERT_TOP_K):
        m = jnp.max(cur, axis=0, keepdims=True)
        idx = jnp.min(jnp.where(cur == m, row, float(e)), axis=0, keepdims=True)
        hit = row == idx
        eidx_o[k:k + 1, :] = idx.astype(I32)
        picked.append(jnp.sum(jnp.where(hit, scores, 0.0), axis=0, keepdims=True))
        cur = jnp.where(hit, NEG_INF, cur)
    total = functools.reduce(lambda a, b2: a + b2, picked)
    for k in range(EXPERT_TOP_K):
        gw_o[k:k + 1, :] = picked[k] / total * ROUTED_SCALE


def _route(logits_t, b_router, *, tn):
    e, n = logits_t.shape
    return pl.pallas_call(
        _route_kernel, grid=(n // tn,),
        in_specs=[pl.BlockSpec((e, tn), lambda i: (0, i)), pl.BlockSpec((e, 1), lambda i: (0, 0))],
        out_specs=[pl.BlockSpec((EXPERT_TOP_K, tn), lambda i: (0, i))] * 2,
        out_shape=[jax.ShapeDtypeStruct((EXPERT_TOP_K, n), I32), jax.ShapeDtypeStruct((EXPERT_TOP_K, n), F32)],
        compiler_params=_cparams("arbitrary"), name="moe_route",
    )(logits_t, b_router.reshape(e, 1))


def _experts_kernel(be_ref, x_ref, wgu_ref, wdn_ref, wrow_ref, y_o, wgu_scr, wdn_scr):
    i = pl.program_id(0)
    de = wdn_ref.shape[0]

    @pl.when((i == 0) | (be_ref[i] != be_ref[jnp.maximum(i - 1, 0)]))
    def _():
        wgu_scr[...] = wgu_ref[...].astype(BF16)
        wdn_scr[...] = wdn_ref[...].astype(BF16)

    gu = _dot(x_ref[...], wgu_scr[...])
    gate = gu[:, :de]
    act = gate * jax.nn.sigmoid(gate) * gu[:, de:]
    y_o[...] = _dot(act.astype(BF16), wdn_scr[...]) * wrow_ref[...]


def _experts(x_sorted, w_rows, block_expert, w_gu, w_dn, layer):
    rows, d = x_sorted.shape
    de = w_dn.shape[2]
    br = MOE_BLOCK_ROWS
    return pl.pallas_call(
        _experts_kernel,
        grid_spec=pltpu.PrefetchScalarGridSpec(
            num_scalar_prefetch=1, grid=(rows // br,),
            in_specs=[pl.BlockSpec((br, d), lambda i, be: (i, 0)),
                      pl.BlockSpec((None, None, d, 2 * de), lambda i, be: (layer, be[i], 0, 0)),
                      pl.BlockSpec((None, None, de, d), lambda i, be: (layer, be[i], 0, 0)),
                      pl.BlockSpec((br, 1), lambda i, be: (i, 0))],
            out_specs=pl.BlockSpec((br, d), lambda i, be: (i, 0)),
            scratch_shapes=[pltpu.VMEM((d, 2 * de), BF16), pltpu.VMEM((de, d), BF16)]),
        out_shape=jax.ShapeDtypeStruct((rows, d), F32),
        compiler_params=_cparams("arbitrary"), name="moe_experts",
    )(block_expert, x_sorted, w_gu, w_dn, w_rows)


def _ffn_out_kernel(h2_ref, routed_ref, x1_ref, gate_ref, gpost_ref, wgu_ref, wdn_ref, y_o):
    ds = wdn_ref.shape[0]
    gu = _dot(h2_ref[...], wgu_ref[...])
    gate = gu[:, :ds]
    act = gate * jax.nn.sigmoid(gate) * gu[:, ds:]
    f = _dot(act.astype(BF16), wdn_ref[...]) + routed_ref[...]
    y_o[...] = x1_ref[...] + gate_ref[0] * _rms(f, gpost_ref[...])


def _ffn_out(h2, routed, x1, gate, mod_map, gpost, wgu, wdn, *, tm):
    n, d = x1.shape
    r = gate.shape[1]
    row = lambda i: (i, 0)
    const = lambda i: (0, 0)
    return pl.pallas_call(
        _ffn_out_kernel, grid=(n // tm,),
        in_specs=[pl.BlockSpec((tm, d), row), pl.BlockSpec((tm, d), row), pl.BlockSpec((tm, d), row),
                  pl.BlockSpec((1, r, d), mod_map), pl.BlockSpec((1, d), const),
                  pl.BlockSpec(wgu.shape, const), pl.BlockSpec(wdn.shape, const)],
        out_specs=pl.BlockSpec((tm, d), row),
        out_shape=jax.ShapeDtypeStruct((n, d), F32),
        compiler_params=_cparams("arbitrary"), name="ffn_out",
    )(h2, routed, x1, gate, gpost.reshape(1, d), wgu, wdn)


def _rope_tables(pos):
    half = HEAD_DIM // 2
    inv_freq = ROPE_THETA ** (-jnp.arange(half, dtype=F32) / half)
    ang = pos.astype(F32)[:, None] * inv_freq[None, :]
    cos, sin = jnp.cos(ang), jnp.sin(ang)
    reps = LANES // HEAD_DIM
    return (jnp.tile(jnp.concatenate([cos, cos], axis=1), (1, reps)),
            jnp.tile(jnp.concatenate([-sin, sin], axis=1), (1, reps)))


def _dispatch(eidx, gw, n_experts, br):
    k, n = eidx.shape
    n_assign = n * k
    n_blocks = -(-(n_assign + n_experts * (br - 1)) // br)
    flat_e = eidx.T.reshape(-1)
    order = jnp.argsort(flat_e)
    sorted_e = flat_e[order]
    counts = jnp.bincount(flat_e, length=n_experts)
    padded = (counts + br - 1) // br * br
    start = jnp.cumsum(counts) - counts
    pad_end = jnp.cumsum(padded)
    pad_start = pad_end - padded
    dest_sorted = (pad_start[sorted_e] + jnp.arange(n_assign) - start[sorted_e]).astype(I32)
    rows = n_blocks * br
    tok_pad = jnp.full((rows,), n, I32).at[dest_sorted].set((order // k).astype(I32))
    w_pad = jnp.zeros((rows,), F32).at[dest_sorted].set(gw.T.reshape(-1)[order])
    dest = jnp.zeros((n_assign,), I32).at[order].set(dest_sorted)
    block_expert = jnp.minimum(jnp.searchsorted(pad_end, jnp.arange(n_blocks) * br, side="right"),
                               n_experts - 1).astype(I32)
    return tok_pad, w_pad, dest, block_expert


def kernel(x_prompt, x_sample, c_prompt, c_sample, cache_a_kv, cache_a_idx, cache_b_kv, page_table, w_ada, b_ada,
           g_pre_mix, g_post_mix, g_pre_ffn, g_post_ffn, w_in, w_proj_a, w_proj_b, w_out, w_router, b_router,
           w_exp_gu, w_exp_down, w_sh_gu, w_sh_down):
    b, s, d = x_prompt.shape
    db, t, _ = x_sample.shape
    n_p, n_s = b * s, db * t
    depth = w_ada.shape[0]
    n_layers, n_phys, page = cache_a_kv.shape[:3]
    past = page_table.shape[1] * page
    n_experts = w_router.shape[2]
    tm_p = ROW_TILE
    tm_s = min(ROW_TILE, n_s)
    assert s % ATT_TILE == 0 and ATT_TILE == tm_p and n_s % tm_s == 0 and (n_p + n_s) % tm_p == 0
    assert page_table.shape[1] % PAGES_PER_STEP == 0 and t <= page

    cos_p, sin_p = _rope_tables(jnp.arange(s, dtype=jnp.int32))
    cos_s, sin_s = _rope_tables(past + jnp.arange(t, dtype=jnp.int32))
    cos_s, sin_s = jnp.tile(cos_s, (db, 1)), jnp.tile(sin_s, (db, 1))
    cache_a = cache_a_kv.reshape(n_layers, n_phys, page, 2 * WIDTH)
    cache_b = cache_b_kv.reshape(n_layers, n_phys, page, 2 * WIDTH)
    seq_tiles = s // tm_p
    map_p = lambda i: (i // seq_tiles, 0, 0)
    map_s = lambda i: (0, i, 0)

    xp, xs = x_prompt.reshape(n_p, d), x_sample.reshape(n_s, d)
    state_p, state_s = [], []
    for l in range(depth):
        mod = _ada(jnp.concatenate([c_prompt, c_sample], axis=0), w_ada[l], b_ada[l]).reshape(b + db, 6, d)
        mod_p = [mod[:b, k][:, None, :] for k in range(6)]
        mod_s = [jnp.repeat(mod[b:, k], t, axis=0)[None] for k in range(6)]

        o = 0
        cols = []
        for width in (WIDTH, 2 * WIDTH, N_IDX_HEADS * IDX_DIM, IDX_DIM + N_IDX_HEADS, WIDTH, 2 * WIDTH, 2 * d):
            cols.append(w_in[l][:, o:o + width])
            o += width
        cols[3] = jnp.pad(cols[3], ((0, 0), (0, LANES - cols[3].shape[1])))
        w_proj = [c.astype(BF16) for c in (cols[0], cols[1], cols[2], cols[3], cols[4], cols[5], cols[6])]
        wr_hi = w_router[l].T.astype(BF16)
        wr_lo = (w_router[l].T - wr_hi.astype(F32)).astype(BF16)
        w_mix = [w_proj_a[l].astype(BF16), w_proj_b[l].astype(BF16), w_out[l].astype(BF16), wr_hi, wr_lo]
        wsh_gu, wsh_dn = w_sh_gu[l].astype(BF16), w_sh_down[l].astype(BF16)

        (qa, qi, qb, kva, kvb, sm, ka, kb, ki, ga, gb, vat, vbt) = _inproj(
            xp, mod_p[1], mod_p[0], map_p, g_pre_mix[l], cos_p, sin_p, seq_tiles, w_proj, tm=tm_p, seq_tiles=seq_tiles)
        oa = _dsa_prompt(qi, qa, sm, ki, ka, vat, b=b, s=s, k_top=min(TOPK_MAX, s // 4))
        ob = _sb_prompt(qb, kb, vbt, b=b, s=s)
        x1_p, h2_p, lg_p = _outproj(oa, ob, ga, gb, xp, mod_p[2], mod_p[4], mod_p[3], map_p,
                                    g_post_mix[l], g_pre_ffn[l], w_mix, tm=tm_p)
        state_p.append((kva.reshape(b, s, 2, N_HEADS, HEAD_DIM), sm[:, :IDX_DIM].reshape(b, s, IDX_DIM),
                        kvb.reshape(b, s, 2, N_HEADS, HEAD_DIM)))

        (qa, qi, qb, kva, kvb, sm, _, _, _, ga, gb) = _inproj(
            xs, mod_s[1], mod_s[0], map_s, g_pre_mix[l], cos_s, sin_s, n_s // tm_s, w_proj, tm=tm_s, seq_tiles=None)
        oa = _dsa_sample(qi.reshape(db, t, WIDTH), qa.reshape(db, t, WIDTH), sm.reshape(db, t, LANES),
                         kva.reshape(db, t, 2 * WIDTH), cache_a_idx, cache_a, page_table, l,
                         k_top=min(TOPK_MAX, (past + t) // 4))
        ob = _sb_sample(qb.reshape(db, t, WIDTH), kvb.reshape(db, t, 2 * WIDTH), cache_b, page_table, l)
        x1_s, h2_s, lg_s = _outproj(oa.reshape(n_s, WIDTH), ob.reshape(n_s, WIDTH), ga, gb, xs, mod_s[2], mod_s[4],
                                    mod_s[3], map_s, g_post_mix[l], g_pre_ffn[l], w_mix, tm=tm_s)
        state_s.append((kva.reshape(db, t, 2, N_HEADS, HEAD_DIM), sm[:, :IDX_DIM].reshape(db, t, IDX_DIM),
                        kvb.reshape(db, t, 2, N_HEADS, HEAD_DIM)))

        h2 = jnp.concatenate([h2_p, h2_s], axis=0)
        eidx, gw = _route(jnp.concatenate([lg_p, lg_s], axis=1), b_router[l], tn=tm_p)
        tok_pad, w_pad, dest, block_expert = _dispatch(eidx, gw, n_experts, MOE_BLOCK_ROWS)
        x_sorted = jnp.concatenate([h2, jnp.zeros((1, d), BF16)], axis=0)[tok_pad]
        y_sorted = _experts(x_sorted, w_pad[:, None], block_expert, w_exp_gu, w_exp_down, l)
        routed = y_sorted[dest].reshape(n_p + n_s, EXPERT_TOP_K, d).sum(axis=1)
        xp = _ffn_out(h2_p, routed[:n_p], x1_p, mod_p[5], map_p, g_post_ffn[l], wsh_gu, wsh_dn, tm=tm_p)
        xs = _ffn_out(h2_s, routed[n_p:], x1_s, mod_s[5], map_s, g_post_ffn[l], wsh_gu, wsh_dn, tm=tm_s)

    stack = lambda states, k: jnp.stack([st[k] for st in states])
    return (xp.reshape(b, s, d), xs.reshape(db, t, d), stack(state_p, 0), stack(state_p, 1), stack(state_p, 2),
            stack(state_s, 0), stack(state_s, 1), stack(state_s, 2))
```

```python
import functools

import jax
import jax.numpy as jnp
from jax import lax
from jax.experimental import pallas as pl
from jax.experimental.pallas import tpu as pltpu

F32 = jnp.float32
BF16 = jnp.bfloat16
I32 = jnp.int32

HEAD_DIM = 64
N_HEADS = 8
WIDTH = N_HEADS * HEAD_DIM
IDX_DIM = 64
N_IDX_HEADS = 8
TOPK_MAX = 256
ROPE_THETA = 10000.0
RMS_EPS = 1e-6
EXPERT_TOP_K = 8
N_EXPERT_GROUPS = 8
TOPK_GROUPS = 4
ROUTED_SCALE = 2.5

LANES = 128
ROW_TILE = 256
ATT_TILE = 256
PAGES_PER_STEP = 8
MOE_BLOCK_ROWS = 128
VMEM_LIMIT = 56 * 1024 * 1024
INT_MIN = -(2 ** 31)
NEG_INF = float("-inf")


def _cparams(*sem):
    return pltpu.CompilerParams(dimension_semantics=sem, vmem_limit_bytes=VMEM_LIMIT)


def _dot(a, b):
    return jnp.dot(a, b, preferred_element_type=F32)


def _dot_nt(a, b):
    return lax.dot_general(a, b, (((1,), (1,)), ((), ())), preferred_element_type=F32)


def _split_bf16(v):
    hi = v.astype(BF16)
    lo = (v - hi.astype(F32)).astype(BF16)
    return hi, lo


def _rms(v, g):
    return v * lax.rsqrt(jnp.mean(v * v, axis=-1, keepdims=True) + RMS_EPS) * g


def _ada_kernel(c_ref, w_ref, b_ref, o_ref):
    c = c_ref[...]
    a_hi, a_lo = _split_bf16(c * jax.nn.sigmoid(c))
    w_hi, w_lo = _split_bf16(w_ref[...])
    o_ref[...] = _dot(a_hi, w_hi) + _dot(a_hi, w_lo) + _dot(a_lo, w_hi) + b_ref[...]


def _ada(c, w, b):
    m, d = c.shape
    n = w.shape[1]
    tn = 1024 if n % 1024 == 0 else n
    return pl.pallas_call(
        _ada_kernel,
        grid=(n // tn,),
        in_specs=[pl.BlockSpec((m, d), lambda j: (0, 0)),
                  pl.BlockSpec((d, tn), lambda j: (0, j)),
                  pl.BlockSpec((1, tn), lambda j: (0, j))],
        out_specs=pl.BlockSpec((m, tn), lambda j: (0, j)),
        out_shape=jax.ShapeDtypeStruct((m, n), F32),
        compiler_params=_cparams("arbitrary"),
        name="ada_mod",
    )(c, w, b.reshape(1, n))


def _rope_cols(v, cos, sin_signed):
    rows = v.shape[0]
    lane = lax.broadcasted_iota(I32, (rows, LANES), 1)
    first_half = (lane % HEAD_DIM) < (HEAD_DIM // 2)
    outs = []
    for c in range(v.shape[1] // LANES):
        ch = v[:, c * LANES:(c + 1) * LANES]
        partner = jnp.where(first_half, pltpu.roll(ch, LANES - HEAD_DIM // 2, 1), pltpu.roll(ch, HEAD_DIM // 2, 1))
        outs.append(ch * cos + partner * sin_signed)
    return outs[0] if len(outs) == 1 else jnp.concatenate(outs, axis=1)


def _inproj_kernel(x_ref, sc_ref, sh_ref, g_ref, cos_ref, sin_ref,
                   wqa_ref, wkva_ref, wqi_ref, wsm_ref, wqb_ref, wkvb_ref, wg_ref,
                   qa_o, qi_o, qb_o, sm_o, ka_o, kb_o, ki_o, ga_o, gb_o, *state_o):
    d = x_ref.shape[1]
    h = _rms(x_ref[...], g_ref[...]) * (1.0 + sc_ref[0]) + sh_ref[0]
    hb = h.astype(BF16)
    cos = cos_ref[...]
    sin = sin_ref[...]

    qa_o[...] = _rope_cols(_dot(hb, wqa_ref[...]), cos, sin).astype(BF16)
    qi_o[...] = _rope_cols(_dot(hb, wqi_ref[...]), cos, sin).astype(BF16)
    qb_o[...] = _dot(hb, wqb_ref[...]).astype(BF16)

    kva = _dot(hb, wkva_ref[...])
    kva = jnp.concatenate([_rope_cols(kva[:, :WIDTH], cos, sin), kva[:, WIDTH:]], axis=1)
    ka_o[...] = kva[:, :WIDTH].astype(BF16)
    kvb = _dot(hb, wkvb_ref[...])
    kb_o[...] = kvb[:, :WIDTH].astype(BF16)

    sm = _dot(hb, wsm_ref[...])
    lane = lax.broadcasted_iota(I32, sm.shape, 1)
    sm = jnp.where(lane < IDX_DIM, _rope_cols(sm, cos, sin), sm)
    sm_o[...] = sm
    ki_o[...] = sm.astype(BF16)

    g = jax.nn.sigmoid(_dot(hb, wg_ref[...]))
    ga_o[...] = g[:, :d]
    gb_o[...] = g[:, d:]

    if len(state_o) == 2:
        kva_o, kvb_o = state_o
        kva_o[...] = kva
        kvb_o[...] = kvb
    else:
        kvat_o, kvbt_o, kit_o, vat_o, vbt_o = state_o
        kva_t, kvb_t = kva.T, kvb.T
        kvat_o[0] = kva_t
        kvbt_o[0] = kvb_t
        kit_o[0] = sm.T[:IDX_DIM, :]
        vat_o[0, 0] = kva_t[WIDTH:, :].astype(BF16)
        vbt_o[0, 0] = kvb_t[WIDTH:, :].astype(BF16)


def _inproj(x, scale, shift, mod_map, g, cos, sin, tbl_blocks, ws, *, tm, seq_tiles):
    n, d = x.shape
    r = scale.shape[1]
    grid = (n // tm,)
    row = lambda i: (i, 0)
    const = lambda i: (0, 0)
    in_specs = [pl.BlockSpec((tm, d), row),
                pl.BlockSpec((1, r, d), mod_map),
                pl.BlockSpec((1, r, d), mod_map),
                pl.BlockSpec((1, d), const),
                pl.BlockSpec((tm, LANES), lambda i: (i % tbl_blocks, 0)),
                pl.BlockSpec((tm, LANES), lambda i: (i % tbl_blocks, 0))]
    in_specs += [pl.BlockSpec(w.shape, const) for w in ws]
    out_cols = [(WIDTH, BF16), (WIDTH, BF16), (WIDTH, BF16), (LANES, F32),
                (WIDTH, BF16), (WIDTH, BF16), (LANES, BF16), (d, F32), (d, F32)]
    if seq_tiles is None:
        out_cols += [(2 * WIDTH, F32), (2 * WIDTH, F32)]
    out_specs = [pl.BlockSpec((tm, c), row) for c, _ in out_cols]
    out_shape = [jax.ShapeDtypeStruct((n, c), t) for c, t in out_cols]
    if seq_tiles is not None:
        nb = n // (tm * seq_tiles)
        seq_map = lambda i: (i // seq_tiles, 0, i % seq_tiles)
        vt_map = lambda i: (i // seq_tiles, i % seq_tiles, 0, 0)
        out_specs += [pl.BlockSpec((1, 2 * WIDTH, tm), seq_map)] * 2 + [pl.BlockSpec((1, IDX_DIM, tm), seq_map)]
        out_shape += [jax.ShapeDtypeStruct((nb, 2 * WIDTH, tm * seq_tiles), F32)] * 2
        out_shape += [jax.ShapeDtypeStruct((nb, IDX_DIM, tm * seq_tiles), F32)]
        out_specs += [pl.BlockSpec((1, 1, WIDTH, tm), vt_map)] * 2
        out_shape += [jax.ShapeDtypeStruct((nb, seq_tiles, WIDTH, tm), BF16)] * 2
    return pl.pallas_call(
        _inproj_kernel, grid=grid, in_specs=in_specs, out_specs=out_specs, out_shape=out_shape,
        compiler_params=_cparams("arbitrary"), name="in_proj",
    )(x, scale, shift, g.reshape(1, d), cos, sin, *ws)


def _sortable(v):
    b = lax.bitcast_convert_type(v, I32)
    return b ^ ((b >> 31) & jnp.int32(0x7FFFFFFF))


def _select_topk(load_keys, n_chunks, part_sum, final_sum, pos_of_chunk, shape1, part_shape, k_top, idx_bits):
    def count(pred):
        def body(c, acc):
            return acc + part_sum(jnp.where(pred(load_keys(c), c), 1.0, 0.0))
        return final_sum(lax.fori_loop(0, n_chunks, body, jnp.zeros(part_shape, F32)))

    def bit_body(bi, t_u):
        cand_u = t_u | jnp.left_shift(jnp.int32(1), 31 - bi)
        cand_s = cand_u ^ jnp.int32(INT_MIN)
        cnt = count(lambda k, c: k >= cand_s)
        return jnp.where(cnt >= k_top, cand_u, t_u)

    t_u = lax.fori_loop(0, 32, bit_body, jnp.zeros(shape1, I32))
    thr = t_u ^ jnp.int32(INT_MIN)
    need = k_top - count(lambda k, c: k > thr)

    def idx_body(bi, cut):
        cand = cut | jnp.left_shift(jnp.int32(1), idx_bits - 1 - bi)
        below = count(lambda k, c: (k == thr) & (pos_of_chunk(c) < cand))
        return jnp.where(below < need, cand, cut)

    n_ge = count(lambda k, c: (k >= thr) & (k != jnp.int32(INT_MIN)))
    split_ties = jnp.max(n_ge) > k_top
    cut = lax.cond(split_ties,
                   lambda: lax.fori_loop(0, idx_bits, idx_body, jnp.zeros(shape1, I32)),
                   lambda: jnp.full(shape1, (1 << idx_bits) - 1, I32))
    return thr, cut


def _dsa_prompt_kernel(qi_ref, qa_ref, sm_ref, ki_ref, ka_ref, vat_ref, o_ref, keys_scr, acc_scr, m_scr, l_scr,
                       *, k_top, idx_bits):
    t = qi_ref.shape[0]
    i = pl.program_id(1)
    n_chunks = i + 1
    q_pos = i * t + lax.broadcasted_iota(I32, (t, t), 1)
    k_iota = lax.broadcasted_iota(I32, (t, t), 0)

    qit = qi_ref[...].astype(F32).T.astype(BF16)
    smt = sm_ref[...].T
    idx_scale = IDX_DIM ** -0.5 * N_IDX_HEADS ** -0.5

    def score_chunk(c, carry):
        off = pl.multiple_of(c * t, t)
        kc = ki_ref[pl.ds(off, t), :][:, :IDX_DIM]
        acc = jnp.zeros((t, t), F32)
        for h in range(N_IDX_HEADS):
            s = _dot(kc, qit[h * IDX_DIM:(h + 1) * IDX_DIM, :])
            acc = acc + jnp.maximum(s, 0.0) * smt[IDX_DIM + h:IDX_DIM + h + 1, :]
        key = _sortable(acc * idx_scale)
        keys_scr[pl.ds(off, t), :] = jnp.where(c * t + k_iota <= q_pos, key, jnp.int32(INT_MIN))
        return carry

    lax.fori_loop(0, n_chunks, score_chunk, 0)

    def load_keys(c):
        return keys_scr[pl.ds(pl.multiple_of(c * t, t), t), :]

    thr, cut = _select_topk(load_keys, n_chunks, lambda x: x.reshape(t // 8, 8, t).sum(axis=0),
                            lambda p: jnp.sum(p, axis=0, keepdims=True), lambda c: c * t + k_iota,
                            (1, t), (8, t), k_top, idx_bits)

    qat = (qa_ref[...].astype(F32) * HEAD_DIM ** -0.5).T.astype(BF16)
    acc_scr[...] = jnp.zeros_like(acc_scr)
    m_scr[...] = jnp.full_like(m_scr, NEG_INF)
    l_scr[...] = jnp.zeros_like(l_scr)

    def att_chunk(c, carry):
        off = pl.multiple_of(c * t, t)
        key = keys_scr[pl.ds(off, t), :]
        sel = ((key > thr) | ((key == thr) & (c * t + k_iota <= cut))) & (key != jnp.int32(INT_MIN))
        k = ka_ref[pl.ds(off, t), :]
        heads = [slice(h * HEAD_DIM, (h + 1) * HEAD_DIM) for h in range(N_HEADS)]
        logits = [jnp.where(sel, _dot(k[:, hs], qat[hs, :]), NEG_INF) for hs in heads]
        m_old = m_scr[...]
        m_new = jnp.maximum(m_old, jnp.concatenate([jnp.max(lt, axis=0, keepdims=True) for lt in logits], axis=0))
        m_safe = jnp.where(m_new == NEG_INF, 0.0, m_new)
        alpha = jnp.exp(m_old - m_safe)
        p = [jnp.exp(lt - m_safe[h:h + 1, :]) for h, lt in enumerate(logits)]
        l_scr[...] = alpha * l_scr[...] + jnp.concatenate([jnp.sum(ph, axis=0, keepdims=True) for ph in p], axis=0)
        m_scr[...] = m_new
        pv = [_dot(vat_ref[c, hs, :], ph.astype(BF16)) for hs, ph in zip(heads, p)]
        for h, hs in enumerate(heads):
            acc_scr[hs, :] = alpha[h:h + 1, :] * acc_scr[hs, :] + pv[h]
        return carry

    lax.fori_loop(0, n_chunks, att_chunk, 0)

    for h in range(N_HEADS):
        hs = slice(h * HEAD_DIM, (h + 1) * HEAD_DIM)
        acc_scr[hs, :] = acc_scr[hs, :] * (1.0 / l_scr[h:h + 1, :])
    o_ref[...] = acc_scr[...].T.astype(BF16)


def _dsa_prompt(qi, qa, sm, ki, ka, vat, *, b, s, k_top):
    t = ATT_TILE
    nq = s // t
    qmap = lambda bi, i: (bi * nq + i, 0)
    kmap = lambda bi, i: (bi, 0)
    return pl.pallas_call(
        functools.partial(_dsa_prompt_kernel, k_top=k_top, idx_bits=max(1, (s - 1).bit_length())),
        grid=(b, nq),
        in_specs=[pl.BlockSpec((t, WIDTH), qmap), pl.BlockSpec((t, WIDTH), qmap), pl.BlockSpec((t, LANES), qmap),
                  pl.BlockSpec((s, LANES), kmap), pl.BlockSpec((s, WIDTH), kmap),
                  pl.BlockSpec((None, nq, WIDTH, t), lambda bi, i: (bi, 0, 0, 0))],
        out_specs=pl.BlockSpec((t, WIDTH), qmap),
        out_shape=jax.ShapeDtypeStruct((b * s, WIDTH), BF16),
        scratch_shapes=[pltpu.VMEM((s, t), I32), pltpu.VMEM((WIDTH, t), F32),
                        pltpu.VMEM((N_HEADS, t), F32), pltpu.VMEM((N_HEADS, t), F32)],
        compiler_params=_cparams("arbitrary", "arbitrary"),
        name="dsa_prompt",
    )(qi, qa, sm, ki, ka, vat)


def _log_sigmoid_pair(z):
    tail = jnp.log(1.0 + jnp.exp(-jnp.abs(z)))
    return jnp.minimum(z, 0.0) - tail, jnp.minimum(-z, 0.0) - tail


def _sb_prompt_kernel(qb_ref, kb_ref, vbt_ref, o_ref, acc_scr, carry_scr):
    t = qb_ref.shape[0]
    i = pl.program_id(1)
    q_pos = i * t + lax.broadcasted_iota(I32, (t, t), 1)
    k_iota = lax.broadcasted_iota(I32, (t, t), 0)
    later = (lax.broadcasted_iota(I32, (t, t), 1) > k_iota).astype(BF16)
    qt = (qb_ref[...].astype(F32) * HEAD_DIM ** -0.5).T.astype(BF16)
    acc_scr[...] = jnp.zeros_like(acc_scr)
    carry_scr[...] = jnp.zeros_like(carry_scr)

    heads = [slice(h * HEAD_DIM, (h + 1) * HEAD_DIM) for h in range(N_HEADS)]

    def chunk(c, mask):
        off = pl.multiple_of(c * t, t)
        k = kb_ref[pl.ds(off, t), :]
        z = [_dot(k[:, hs], qt[hs, :]) for hs in heads]
        pairs = [_log_sigmoid_pair(zh) for zh in z]
        log_keep = [lk if mask is None else jnp.where(mask, lk, 0.0) for _, lk in pairs]
        split = [_split_bf16(lk) for lk in log_keep]
        suffix = [_dot(later, hi) + _dot(later, lo) + carry_scr[h:h + 1, :] for h, (hi, lo) in enumerate(split)]
        w = [jnp.exp(lb + sf) for (lb, _), sf in zip(pairs, suffix)]
        if mask is not None:
            w = [jnp.where(mask, wh, 0.0) for wh in w]
        pv = [_dot(vbt_ref[c, hs, :], wh.astype(BF16)) for hs, wh in zip(heads, w)]
        for h, hs in enumerate(heads):
            acc_scr[hs, :] = acc_scr[hs, :] + pv[h]
            carry_scr[h:h + 1, :] = carry_scr[h:h + 1, :] + jnp.sum(log_keep[h], axis=0, keepdims=True)

    chunk(i, i * t + k_iota < q_pos)

    def earlier_chunk(j, carry):
        chunk(i - 1 - j, None)
        return carry

    lax.fori_loop(0, i, earlier_chunk, 0)
    o_ref[...] = acc_scr[...].T.astype(BF16)


def _sb_prompt(qb, kb, vbt, *, b, s):
    t = ATT_TILE
    nq = s // t
    qmap = lambda bi, i: (bi * nq + i, 0)
    return pl.pallas_call(
        _sb_prompt_kernel,
        grid=(b, nq),
        in_specs=[pl.BlockSpec((t, WIDTH), qmap), pl.BlockSpec((s, WIDTH), lambda bi, i: (bi, 0)),
                  pl.BlockSpec((None, nq, WIDTH, t), lambda bi, i: (bi, 0, 0, 0))],
        out_specs=pl.BlockSpec((t, WIDTH), qmap),
        out_shape=jax.ShapeDtypeStruct((b * s, WIDTH), BF16),
        scratch_shapes=[pltpu.VMEM((WIDTH, t), F32), pltpu.VMEM((N_HEADS, t), F32)],
        compiler_params=_cparams("arbitrary", "arbitrary"),
        name="sb_prompt",
    )(qb, kb, vbt)


def _rows_by_head(q):
    return jnp.concatenate([q[:, h * HEAD_DIM:(h + 1) * HEAD_DIM] for h in range(N_HEADS)], axis=0)


def _block_diag_rows(q):
    t = q.shape[0]
    rows = jnp.concatenate([q] * N_HEADS, axis=0)
    r = lax.broadcasted_iota(I32, rows.shape, 0) // t
    c = lax.broadcasted_iota(I32, rows.shape, 1) // HEAD_DIM
    return jnp.where(r == c, rows, 0.0)


def _diag_heads(acc, t):
    return jnp.concatenate([acc[h * t:(h + 1) * t, h * HEAD_DIM:(h + 1) * HEAD_DIM] for h in range(N_HEADS)], axis=1)


def _pad_rows(v, rows):
    return jnp.concatenate([v, jnp.zeros((rows - v.shape[0], v.shape[1]), v.dtype)], axis=0)


def _pages_kv_t(pages):
    k_t = jnp.concatenate([pg[:WIDTH, :] for pg in pages], axis=1).astype(BF16)
    v_t = jnp.concatenate([pg[WIDTH:, :] for pg in pages], axis=1).astype(BF16)
    return k_t, v_t


def _new_kv_t(kv_new, page):
    kv_t = _pad_rows(kv_new, page).T.astype(BF16)
    return kv_t[:WIDTH, :], kv_t[WIDTH:, :]


def _dsa_sample_select_kernel(pt_ref, *refs, k_top, idx_bits, n_steps):
    pages = refs[:PAGES_PER_STEP]
    qi_ref, sm_ref, bias_o, keys_scr = refs[PAGES_PER_STEP:]
    j = pl.program_id(1)
    t = qi_ref.shape[0]
    page = pages[0].shape[1]
    cw = PAGES_PER_STEP * page
    idx_scale = IDX_DIM ** -0.5 * N_IDX_HEADS ** -0.5
    q_rows = _rows_by_head(qi_ref[...].astype(F32)).astype(BF16)
    sm = sm_ref[...]
    w_cols = [jnp.broadcast_to(sm[:, IDX_DIM + h:IDX_DIM + h + 1], (t, page)) for h in range(N_IDX_HEADS)]

    def scores(k_t):
        s = _dot(q_rows, k_t)
        acc = jnp.zeros((t, page), F32)
        for h in range(N_IDX_HEADS):
            acc = acc + jnp.maximum(s[h * t:(h + 1) * t, :], 0.0) * w_cols[h]
        return acc * idx_scale

    for r in range(PAGES_PER_STEP):
        keys_scr[j, :, r * page:(r + 1) * page] = _sortable(scores(pages[r][...].astype(BF16)))

    @pl.when(j == n_steps - 1)
    def _():
        k_new = _pad_rows(sm[:, :IDX_DIM], page).T.astype(BF16)
        key = _sortable(scores(k_new))
        u = lax.broadcasted_iota(I32, (t, page), 1)
        q = lax.broadcasted_iota(I32, (t, page), 0)
        keys_scr[n_steps, :, :page] = jnp.where(u <= q, key, jnp.int32(INT_MIN))
        keys_scr[n_steps, :, page:] = jnp.full((t, cw - page), INT_MIN, I32)

        lane = lax.broadcasted_iota(I32, (t, cw), 1)
        part = lambda x: functools.reduce(lambda a, b2: a + b2, [x[:, k * page:(k + 1) * page] for k in range(PAGES_PER_STEP)])
        final = lambda p: jnp.sum(p, axis=1, keepdims=True)
        thr, cut = _select_topk(lambda c: keys_scr[c], n_steps + 1, part, final, lambda c: c * cw + lane,
                                (t, 1), (t, page), k_top, idx_bits)

        def write(c, carry):
            key = keys_scr[c]
            sel = ((key > thr) | ((key == thr) & (c * cw + lane <= cut))) & (key != jnp.int32(INT_MIN))
            bias_o[c] = jnp.where(sel, 0.0, NEG_INF)
            return carry

        lax.fori_loop(0, n_steps + 1, write, 0)


def _dsa_sample_attend_kernel(pt_ref, *refs, n_steps):
    pages = refs[:PAGES_PER_STEP]
    qa_ref, kv_new_ref, bias_ref, bias_new_ref, o_ref, acc_scr, m_scr, l_scr = refs[PAGES_PER_STEP:]
    j = pl.program_id(1)
    t = qa_ref.shape[0]
    page = pages[0].shape[1]
    q_bd = _block_diag_rows(qa_ref[...].astype(F32) * HEAD_DIM ** -0.5).astype(BF16)

    @pl.when(j == 0)
    def _():
        acc_scr[...] = jnp.zeros_like(acc_scr)
        m_scr[...] = jnp.full_like(m_scr, NEG_INF)
        l_scr[...] = jnp.zeros_like(l_scr)

    def update(k_t, v_t, bias):
        logits = _dot(q_bd, k_t) + jnp.concatenate([bias] * N_HEADS, axis=0)
        m_old = m_scr[...]
        m_new = jnp.maximum(m_old, jnp.max(logits, axis=1, keepdims=True))
        m_safe = jnp.where(m_new == NEG_INF, 0.0, m_new)
        alpha = jnp.exp(m_old - m_safe)
        p = jnp.exp(logits - m_safe)
        l_scr[...] = alpha * l_scr[...] + jnp.sum(p, axis=1, keepdims=True)
        m_scr[...] = m_new
        acc_scr[...] = alpha * acc_scr[...] + _dot_nt(p.astype(BF16), v_t)

    update(*_pages_kv_t(pages), bias_ref[...])

    @pl.when(j == n_steps - 1)
    def _():
        update(*_new_kv_t(kv_new_ref[...], page), bias_new_ref[:, :page])
        o_ref[...] = _diag_heads(acc_scr[...] * (1.0 / l_scr[...]), t).astype(BF16)


def _page_specs(cache, layer, n_pages, reverse=False):
    page, width = cache.shape[2:]

    def spec(r):
        if reverse:
            return pl.BlockSpec((None, None, page, width),
                                lambda b, j, pt: (layer, pt[b, n_pages - 1 - (j * PAGES_PER_STEP + r)], 0, 0))
        return pl.BlockSpec((None, None, page, width), lambda b, j, pt: (layer, pt[b, j * PAGES_PER_STEP + r], 0, 0))
    return [spec(r) for r in range(PAGES_PER_STEP)]


def _dsa_sample(qi, qa, sm, kva_new, cache_idx, cache_kv, page_table, layer, *, k_top):
    db, t, _ = qi.shape
    n_pages = page_table.shape[1]
    n_steps = n_pages // PAGES_PER_STEP
    page = cache_idx.shape[3]
    cw = PAGES_PER_STEP * page
    per_b = lambda width: pl.BlockSpec((None, t, width), lambda b, j, pt: (b, 0, 0))
    n_keys = n_pages * page + t
    bias = pl.pallas_call(
        functools.partial(_dsa_sample_select_kernel, k_top=k_top, idx_bits=max(1, (n_keys - 1).bit_length()), n_steps=n_steps),
        grid_spec=pltpu.PrefetchScalarGridSpec(
            num_scalar_prefetch=1, grid=(db, n_steps),
            in_specs=_page_specs(cache_idx, layer, n_pages) + [per_b(WIDTH), per_b(LANES)],
            out_specs=pl.BlockSpec((None, n_steps + 1, t, cw), lambda b, j, pt: (b, 0, 0, 0)),
            scratch_shapes=[pltpu.VMEM((n_steps + 1, t, cw), I32)]),
        out_shape=jax.ShapeDtypeStruct((db, n_steps + 1, t, cw), F32),
        compiler_params=_cparams("arbitrary", "arbitrary"),
        name="dsa_sample_select",
    )(page_table, *([cache_idx] * PAGES_PER_STEP), qi, sm)
    return pl.pallas_call(
        functools.partial(_dsa_sample_attend_kernel, n_steps=n_steps),
        grid_spec=pltpu.PrefetchScalarGridSpec(
            num_scalar_prefetch=1, grid=(db, n_steps),
            in_specs=_page_specs(cache_kv, layer, n_pages) + [
                per_b(WIDTH), per_b(2 * WIDTH),
                pl.BlockSpec((None, None, t, cw), lambda b, j, pt: (b, j, 0, 0)),
                pl.BlockSpec((None, None, t, cw), lambda b, j, pt: (b, n_steps, 0, 0))],
            out_specs=per_b(WIDTH),
            scratch_shapes=[pltpu.VMEM((N_HEADS * t, WIDTH), F32), pltpu.VMEM((N_HEADS * t, 1), F32),
                            pltpu.VMEM((N_HEADS * t, 1), F32)]),
        out_shape=jax.ShapeDtypeStruct((db, t, WIDTH), BF16),
        compiler_params=_cparams("arbitrary", "arbitrary"),
        name="dsa_sample_attend",
    )(page_table, *([cache_kv] * PAGES_PER_STEP), qa, kva_new, bias, bias)


def _suffix_scan_lanes(x):
    lane = lax.broadcasted_iota(I32, x.shape, 1)
    sh = 1
    while sh < LANES:
        x = x + jnp.where(lane + sh < LANES, pltpu.roll(x, LANES - sh, 1), 0.0)
        sh *= 2
    return x


def _sb_sample_kernel(pt_ref, *refs, n_steps):
    pages = refs[:PAGES_PER_STEP]
    qb_ref, kv_new_ref, o_ref, acc_scr, carry_scr = refs[PAGES_PER_STEP:]
    j = pl.program_id(1)
    t = qb_ref.shape[0]
    page = pages[0].shape[1]
    q_bd = _block_diag_rows(qb_ref[...].astype(F32) * HEAD_DIM ** -0.5).astype(BF16)

    def weights(z, mask):
        outs = []
        carry = carry_scr[...]
        for k in range(z.shape[1] // page):
            zt = z[:, k * page:(k + 1) * page]
            log_beta, log_keep = _log_sigmoid_pair(zt)
            if mask is not None:
                log_keep = jnp.where(mask, log_keep, 0.0)
            incl = _suffix_scan_lanes(log_keep)
            w = jnp.exp(log_beta + (incl - log_keep) + carry)
            if mask is not None:
                w = jnp.where(mask, w, 0.0)
            outs.append(w)
            carry = carry + incl[:, :1]
        carry_scr[...] = carry
        return outs[0] if len(outs) == 1 else jnp.concatenate(outs, axis=1)

    @pl.when(j == 0)
    def _():
        carry_scr[...] = jnp.zeros_like(carry_scr)
        k_t, v_t = _new_kv_t(kv_new_ref[...], page)
        u = lax.broadcasted_iota(I32, (N_HEADS * t, page), 1)
        q = lax.broadcasted_iota(I32, (N_HEADS * t, page), 0) % t
        w = weights(_dot(q_bd, k_t), u < q)
        acc_scr[...] = _dot_nt(w.astype(BF16), v_t)

    k_t, v_t = _pages_kv_t(pages)
    w = weights(_dot(q_bd, k_t), None)
    acc_scr[...] = acc_scr[...] + _dot_nt(w.astype(BF16), v_t)

    @pl.when(j == n_steps - 1)
    def _():
        o_ref[...] = _diag_heads(acc_scr[...], t).astype(BF16)


def _sb_sample(qb, kvb_new, cache_kv, page_table, layer):
    db, t, _ = qb.shape
    n_pages = page_table.shape[1]
    n_steps = n_pages // PAGES_PER_STEP
    per_b = lambda width: pl.BlockSpec((None, t, width), lambda b, j, pt: (b, 0, 0))
    return pl.pallas_call(
        functools.partial(_sb_sample_kernel, n_steps=n_steps),
        grid_spec=pltpu.PrefetchScalarGridSpec(
            num_scalar_prefetch=1, grid=(db, n_steps),
            in_specs=_page_specs(cache_kv, layer, n_pages, reverse=True) + [per_b(WIDTH), per_b(2 * WIDTH)],
            out_specs=per_b(WIDTH),
            scratch_shapes=[pltpu.VMEM((N_HEADS * t, WIDTH), F32), pltpu.VMEM((N_HEADS * t, 1), F32)]),
        out_shape=jax.ShapeDtypeStruct((db, t, WIDTH), BF16),
        compiler_params=_cparams("arbitrary", "arbitrary"),
        name="sb_sample",
    )(page_table, *([cache_kv] * PAGES_PER_STEP), qb, kvb_new)


def _outproj_kernel(oa_ref, ob_ref, ga_ref, gb_ref, x_ref, gate_ref, sc_ref, sh_ref, gpost_ref, gpre_ref,
                    wpa_ref, wpb_ref, wout_ref, wrh_ref, wrl_ref, x1_o, h2_o, lg_o):
    mix = ga_ref[...] * _dot(oa_ref[...], wpa_ref[...]) + gb_ref[...] * _dot(ob_ref[...], wpb_ref[...])
    mo = _dot(mix.astype(BF16), wout_ref[...])
    x1 = x_ref[...] + gate_ref[0] * _rms(mo, gpost_ref[...])
    x1_o[...] = x1
    h2 = _rms(x1, gpre_ref[...]) * (1.0 + sc_ref[0]) + sh_ref[0]
    h_hi, h_lo = _split_bf16(h2)
    h2_o[...] = h_hi
    lg_o[...] = _dot_nt(wrh_ref[...], h_hi) + _dot_nt(wrh_ref[...], h_lo) + _dot_nt(wrl_ref[...], h_hi)


def _outproj(oa, ob, ga, gb, x, gate, scale, shift, mod_map, gpost, gpre, ws, *, tm):
    n, d = x.shape
    r = gate.shape[1]
    e = ws[3].shape[0]
    row = lambda i: (i, 0)
    const = lambda i: (0, 0)
    in_specs = [pl.BlockSpec((tm, WIDTH), row), pl.BlockSpec((tm, WIDTH), row),
                pl.BlockSpec((tm, d), row), pl.BlockSpec((tm, d), row), pl.BlockSpec((tm, d), row),
                pl.BlockSpec((1, r, d), mod_map), pl.BlockSpec((1, r, d), mod_map), pl.BlockSpec((1, r, d), mod_map),
                pl.BlockSpec((1, d), const), pl.BlockSpec((1, d), const)]
    in_specs += [pl.BlockSpec(w.shape, const) for w in ws]
    return pl.pallas_call(
        _outproj_kernel, grid=(n // tm,), in_specs=in_specs,
        out_specs=[pl.BlockSpec((tm, d), row), pl.BlockSpec((tm, d), row), pl.BlockSpec((e, tm), lambda i: (0, i))],
        out_shape=[jax.ShapeDtypeStruct((n, d), F32), jax.ShapeDtypeStruct((n, d), BF16),
                   jax.ShapeDtypeStruct((e, n), F32)],
        compiler_params=_cparams("arbitrary"), name="out_proj",
    )(oa, ob, ga, gb, x, gate, scale, shift, gpost.reshape(1, d), gpre.reshape(1, d), *ws)


def _route_kernel(lg_ref, b_ref, eidx_o, gw_o, rank_o, count_o, count_scr):
    @pl.when(pl.program_id(0) == 0)
    def _():
        count_scr[...] = jnp.zeros_like(count_scr)

    scores = jax.nn.sigmoid(lg_ref[...])
    biased = scores + b_ref[...]
    e, tn = scores.shape
    gsz = e // N_EXPERT_GROUPS
    group_score = []
    for g in range(N_EXPERT_GROUPS):
        xg = biased[g * gsz:(g + 1) * gsz, :]
        m1 = jnp.max(xg, axis=0, keepdims=True)
        n1 = jnp.sum(jnp.where(xg == m1, 1.0, 0.0), axis=0, keepdims=True)
        m2 = jnp.max(jnp.where(xg < m1, xg, NEG_INF), axis=0, keepdims=True)
        group_score.append(m1 + jnp.where(n1 >= 2.0, m1, m2))
    cur = []
    for g in range(N_EXPERT_GROUPS):
        rank = jnp.zeros((1, tn), F32)
        for g2 in range(N_EXPERT_GROUPS):
            if g2 == g:
                continue
            ahead = (group_score[g2] > group_score[g]) | ((group_score[g2] == group_score[g]) & (g2 < g))
            rank = rank + jnp.where(ahead, 1.0, 0.0)
        cur.append(jnp.where(rank < TOPK_GROUPS, biased[g * gsz:(g + 1) * gsz, :], NEG_INF))
    cur = jnp.concatenate(cur, axis=0)
    row = lax.broadcasted_iota(I32, (e, tn), 0).astype(F32)
    picked, hits = [], []
    for k in range(EXPERT_TOP_K):
        m = jnp.max(cur, axis=0, keepdims=True)
        idx = jnp.min(jnp.where(cur == m, row, float(e)), axis=0, keepdims=True)
        hit = row == idx
        hits.append(hit)
        eidx_o[k:k + 1, :] = idx.astype(I32)
        picked.append(jnp.sum(jnp.where(hit, scores, 0.0), axis=0, keepdims=True))
        cur = jnp.where(hit, NEG_INF, cur)
    total = functools.reduce(lambda a, b2: a + b2, picked)
    for k in range(EXPERT_TOP_K):
        gw_o[k:k + 1, :] = picked[k] / total * ROUTED_SCALE

    onehot = functools.reduce(lambda a, b2: a | b2, hits)
    earlier = lax.broadcasted_iota(I32, (tn, tn), 0) < lax.broadcasted_iota(I32, (tn, tn), 1)
    before = _dot(jnp.where(onehot, 1.0, 0.0).astype(BF16), earlier.astype(BF16)) + count_scr[...]
    for k in range(EXPERT_TOP_K):
        rank_o[k:k + 1, :] = jnp.sum(jnp.where(hits[k], before, 0.0), axis=0, keepdims=True).astype(I32)
    count_scr[...] = count_scr[...] + jnp.sum(jnp.where(onehot, 1.0, 0.0), axis=1, keepdims=True)
    count_o[...] = count_scr[...]


def _route(logits_t, b_router, *, tn):
    e, n = logits_t.shape
    tok = pl.BlockSpec((EXPERT_TOP_K, tn), lambda i: (0, i))
    return pl.pallas_call(
        _route_kernel, grid=(n // tn,),
        in_specs=[pl.BlockSpec((e, tn), lambda i: (0, i)), pl.BlockSpec((e, 1), lambda i: (0, 0))],
        out_specs=[tok, tok, tok, pl.BlockSpec((e, 1), lambda i: (0, 0))],
        out_shape=[jax.ShapeDtypeStruct((EXPERT_TOP_K, n), I32), jax.ShapeDtypeStruct((EXPERT_TOP_K, n), F32),
                   jax.ShapeDtypeStruct((EXPERT_TOP_K, n), I32), jax.ShapeDtypeStruct((e, 1), F32)],
        scratch_shapes=[pltpu.VMEM((e, 1), F32)],
        compiler_params=_cparams("arbitrary"), name="moe_route",
    )(logits_t, b_router.reshape(e, 1))


def _moe_dest_kernel(eidx_ref, rank_ref, start_ref, dest_o):
    e = start_ref.shape[0]
    tn = eidx_ref.shape[1]
    row = lax.broadcasted_iota(I32, (e, tn), 0)
    start = start_ref[...]
    for k in range(EXPERT_TOP_K):
        base = jnp.sum(jnp.where(row == eidx_ref[k:k + 1, :], start, 0.0), axis=0, keepdims=True)
        dest_o[k:k + 1, :] = base.astype(I32) + rank_ref[k:k + 1, :]


def _moe_dest(eidx, rank, pad_start, *, tn):
    k, n = eidx.shape
    e = pad_start.shape[0]
    tok = pl.BlockSpec((k, tn), lambda i: (0, i))
    return pl.pallas_call(
        _moe_dest_kernel, grid=(n // tn,),
        in_specs=[tok, tok, pl.BlockSpec((e, 1), lambda i: (0, 0))],
        out_specs=tok, out_shape=jax.ShapeDtypeStruct((k, n), I32),
        compiler_params=_cparams("arbitrary"), name="moe_dest",
    )(eidx, rank, pad_start)


def _experts_kernel(be_ref, nb_ref, x_ref, wgu_ref, wdn_ref, y_o, wgu_scr, wdn_scr):
    i = pl.program_id(0)
    de = wdn_ref.shape[0]

    @pl.when((i == 0) | (be_ref[i] != be_ref[jnp.maximum(i - 1, 0)]))
    def _():
        wgu_scr[...] = wgu_ref[...].astype(BF16)
        wdn_scr[...] = wdn_ref[...].astype(BF16)

    @pl.when(i < nb_ref[0])
    def _():
        gu = _dot(x_ref[...], wgu_scr[...])
        gate = gu[:, :de]
        act = gate * jax.nn.sigmoid(gate) * gu[:, de:]
        y_o[...] = _dot(act.astype(BF16), wdn_scr[...])

    @pl.when(i >= nb_ref[0])
    def _():
        y_o[...] = jnp.zeros_like(y_o)


def _experts(x_sorted, block_expert, n_used_blocks, w_gu, w_dn, layer):
    rows, d = x_sorted.shape
    de = w_dn.shape[2]
    br = MOE_BLOCK_ROWS
    return pl.pallas_call(
        _experts_kernel,
        grid_spec=pltpu.PrefetchScalarGridSpec(
            num_scalar_prefetch=2, grid=(rows // br,),
            in_specs=[pl.BlockSpec((br, d), lambda i, be, nb: (i, 0)),
                      pl.BlockSpec((None, None, d, 2 * de), lambda i, be, nb: (layer, be[i], 0, 0)),
                      pl.BlockSpec((None, None, de, d), lambda i, be, nb: (layer, be[i], 0, 0))],
            out_specs=pl.BlockSpec((br, d), lambda i, be, nb: (i, 0)),
            scratch_shapes=[pltpu.VMEM((d, 2 * de), BF16), pltpu.VMEM((de, d), BF16)]),
        out_shape=jax.ShapeDtypeStruct((rows, d), F32),
        compiler_params=_cparams("arbitrary"), name="moe_experts",
    )(block_expert, n_used_blocks, x_sorted, w_gu, w_dn)


def _ffn_out_kernel(h2_ref, yg_ref, gw_ref, x1_ref, gate_ref, gpost_ref, wgu_ref, wdn_ref, y_o):
    ds = wdn_ref.shape[0]
    gu = _dot(h2_ref[...], wgu_ref[...])
    gate = gu[:, :ds]
    act = gate * jax.nn.sigmoid(gate) * gu[:, ds:]
    f = _dot(act.astype(BF16), wdn_ref[...])
    gw = gw_ref[...]
    for k in range(EXPERT_TOP_K):
        f = f + yg_ref[k] * gw[:, k:k + 1]
    y_o[...] = x1_ref[...] + gate_ref[0] * _rms(f, gpost_ref[...])


def _ffn_out(h2, y_gathered, gw_t, x1, gate, mod_map, gpost, wgu, wdn, *, tm, row0=0):
    n, d = x1.shape
    r = gate.shape[1]
    k = y_gathered.shape[0]
    blk0 = row0 // tm
    row = lambda i: (i, 0)
    const = lambda i: (0, 0)
    return pl.pallas_call(
        _ffn_out_kernel, grid=(n // tm,),
        in_specs=[pl.BlockSpec((tm, d), row), pl.BlockSpec((k, tm, d), lambda i: (0, blk0 + i, 0)),
                  pl.BlockSpec((tm, k), lambda i: (blk0 + i, 0)), pl.BlockSpec((tm, d), row),
                  pl.BlockSpec((1, r, d), mod_map), pl.BlockSpec((1, d), const),
                  pl.BlockSpec(wgu.shape, const), pl.BlockSpec(wdn.shape, const)],
        out_specs=pl.BlockSpec((tm, d), row),
        out_shape=jax.ShapeDtypeStruct((n, d), F32),
        compiler_params=_cparams("arbitrary"), name="ffn_out",
    )(h2, y_gathered, gw_t, x1, gate, gpost.reshape(1, d), wgu, wdn)


def _rope_tables(pos):
    half = HEAD_DIM // 2
    inv_freq = ROPE_THETA ** (-jnp.arange(half, dtype=F32) / half)
    ang = pos.astype(F32)[:, None] * inv_freq[None, :]
    cos, sin = jnp.cos(ang), jnp.sin(ang)
    reps = LANES // HEAD_DIM
    return (jnp.tile(jnp.concatenate([cos, cos], axis=1), (1, reps)),
            jnp.tile(jnp.concatenate([-sin, sin], axis=1), (1, reps)))


def _expert_layout(counts, n_blocks, br):
    n_experts = counts.shape[0]
    padded = (counts[:, 0].astype(I32) + br - 1) // br * br
    pad_end = jnp.cumsum(padded)
    block_start = jnp.arange(n_blocks, dtype=I32) * br
    block_expert = jnp.minimum(jnp.sum((pad_end[None, :] <= block_start[:, None]).astype(I32), axis=1), n_experts - 1)
    return (pad_end - padded).astype(F32)[:, None], block_expert, (pad_end[-1:] // br).astype(I32)


def kernel(x_prompt, x_sample, c_prompt, c_sample, cache_a_kv, cache_a_idx, cache_b_kv, page_table, w_ada, b_ada,
           g_pre_mix, g_post_mix, g_pre_ffn, g_post_ffn, w_in, w_proj_a, w_proj_b, w_out, w_router, b_router,
           w_exp_gu, w_exp_down, w_sh_gu, w_sh_down):
    b, s, d = x_prompt.shape
    db, t, _ = x_sample.shape
    n_p, n_s = b * s, db * t
    depth = w_ada.shape[0]
    n_layers, n_phys, page = cache_a_kv.shape[:3]
    past = page_table.shape[1] * page
    n_experts = w_router.shape[2]
    tm_p = ROW_TILE
    tm_s = min(ROW_TILE, n_s)
    assert s % ATT_TILE == 0 and ATT_TILE == tm_p and n_s % tm_s == 0 and (n_p + n_s) % tm_p == 0
    assert page_table.shape[1] % PAGES_PER_STEP == 0 and t <= page

    cos_p, sin_p = _rope_tables(jnp.arange(s, dtype=jnp.int32))
    cos_s, sin_s = _rope_tables(past + jnp.arange(t, dtype=jnp.int32))
    cos_s, sin_s = jnp.tile(cos_s, (db, 1)), jnp.tile(sin_s, (db, 1))
    cache_a = cache_a_kv.transpose(0, 1, 3, 4, 5, 2).reshape(n_layers, n_phys, 2 * WIDTH, page)
    cache_b = cache_b_kv.transpose(0, 1, 3, 4, 5, 2).reshape(n_layers, n_phys, 2 * WIDTH, page)
    cache_i = cache_a_idx.transpose(0, 1, 3, 2)
    seq_tiles = s // tm_p
    map_p = lambda i: (i // seq_tiles, 0, 0)
    map_s = lambda i: (0, i, 0)

    xp, xs = x_prompt.reshape(n_p, d), x_sample.reshape(n_s, d)
    state_p, state_s = [], []
    for l in range(depth):
        mod = _ada(jnp.concatenate([c_prompt, c_sample], axis=0), w_ada[l], b_ada[l]).reshape(b + db, 6, d)
        mod_p = [mod[:b, k][:, None, :] for k in range(6)]
        mod_s = [jnp.repeat(mod[b:, k], t, axis=0)[None] for k in range(6)]

        o = 0
        cols = []
        for width in (WIDTH, 2 * WIDTH, N_IDX_HEADS * IDX_DIM, IDX_DIM + N_IDX_HEADS, WIDTH, 2 * WIDTH, 2 * d):
            cols.append(w_in[l][:, o:o + width])
            o += width
        cols[3] = jnp.pad(cols[3], ((0, 0), (0, LANES - cols[3].shape[1])))
        w_proj = [c.astype(BF16) for c in (cols[0], cols[1], cols[2], cols[3], cols[4], cols[5], cols[6])]
        wr_hi = w_router[l].T.astype(BF16)
        wr_lo = (w_router[l].T - wr_hi.astype(F32)).astype(BF16)
        w_mix = [w_proj_a[l].astype(BF16), w_proj_b[l].astype(BF16), w_out[l].astype(BF16), wr_hi, wr_lo]
        wsh_gu, wsh_dn = w_sh_gu[l].astype(BF16), w_sh_down[l].astype(BF16)

        (qa, qi, qb, sm, ka, kb, ki, ga, gb, kva_t, kvb_t, ki_t, vat, vbt) = _inproj(
            xp, mod_p[1], mod_p[0], map_p, g_pre_mix[l], cos_p, sin_p, seq_tiles, w_proj, tm=tm_p, seq_tiles=seq_tiles)
        oa = _dsa_prompt(qi, qa, sm, ki, ka, vat, b=b, s=s, k_top=min(TOPK_MAX, s // 4))
        ob = _sb_prompt(qb, kb, vbt, b=b, s=s)
        x1_p, h2_p, lg_p = _outproj(oa, ob, ga, gb, xp, mod_p[2], mod_p[4], mod_p[3], map_p,
                                    g_post_mix[l], g_pre_ffn[l], w_mix, tm=tm_p)
        seq_first = lambda v: v.reshape(b, 2, N_HEADS, HEAD_DIM, s).transpose(0, 4, 1, 2, 3)
        state_p.append((seq_first(kva_t), ki_t.transpose(0, 2, 1), seq_first(kvb_t)))

        (qa, qi, qb, sm, _, _, _, ga, gb, kva, kvb) = _inproj(
            xs, mod_s[1], mod_s[0], map_s, g_pre_mix[l], cos_s, sin_s, n_s // tm_s, w_proj, tm=tm_s, seq_tiles=None)
        oa = _dsa_sample(qi.reshape(db, t, WIDTH), qa.reshape(db, t, WIDTH), sm.reshape(db, t, LANES),
                         kva.reshape(db, t, 2 * WIDTH), cache_i, cache_a, page_table, l,
                         k_top=min(TOPK_MAX, (past + t) // 4))
        ob = _sb_sample(qb.reshape(db, t, WIDTH), kvb.reshape(db, t, 2 * WIDTH), cache_b, page_table, l)
        x1_s, h2_s, lg_s = _outproj(oa.reshape(n_s, WIDTH), ob.reshape(n_s, WIDTH), ga, gb, xs, mod_s[2], mod_s[4],
                                    mod_s[3], map_s, g_post_mix[l], g_pre_ffn[l], w_mix, tm=tm_s)
        state_s.append((kva.reshape(db, t, 2, N_HEADS, HEAD_DIM), sm[:, :IDX_DIM].reshape(db, t, IDX_DIM),
                        kvb.reshape(db, t, 2, N_HEADS, HEAD_DIM)))

        n_all = n_p + n_s
        br = MOE_BLOCK_ROWS
        n_blocks = -(-(n_all * EXPERT_TOP_K + n_experts * (br - 1)) // br)
        eidx, gw, rank, counts = _route(jnp.concatenate([lg_p, lg_s], axis=1), b_router[l], tn=tm_p)
        pad_start, block_expert, n_used = _expert_layout(counts, n_blocks, br)
        dest = _moe_dest(eidx, rank, pad_start, tn=tm_p).reshape(-1)
        token = jnp.broadcast_to(jnp.arange(n_all, dtype=I32), (EXPERT_TOP_K, n_all)).reshape(-1)
        row_token = jnp.full((n_blocks * br,), n_all, I32).at[dest].set(token, unique_indices=True)
        h2_pad = jnp.concatenate([h2_p, h2_s, jnp.zeros((1, d), BF16)], axis=0)
        y_sorted = _experts(h2_pad[row_token], block_expert, n_used, w_exp_gu, w_exp_down, l)
        y_tok = y_sorted[dest].reshape(EXPERT_TOP_K, n_all, d)
        gw_t = gw.T
        xp = _ffn_out(h2_p, y_tok, gw_t, x1_p, mod_p[5], map_p, g_post_ffn[l], wsh_gu, wsh_dn, tm=tm_p)
        xs = _ffn_out(h2_s, y_tok, gw_t, x1_s, mod_s[5], map_s, g_post_ffn[l], wsh_gu, wsh_dn, tm=tm_s, row0=n_p)

    stack = lambda states, k: jnp.stack([st[k] for st in states])
    return (xp.reshape(b, s, d), xs.reshape(db, t, d), stack(state_p, 0), stack(state_p, 1), stack(state_p, 2),
            stack(state_s, 0), stack(state_s, 1), stack(state_s, 2))
```

```python
import functools

import jax
import jax.numpy as jnp
from jax import lax
from jax.experimental import pallas as pl
from jax.experimental.pallas import tpu as pltpu
from jax.experimental.pallas import tpu_sc as plsc

F32 = jnp.float32
BF16 = jnp.bfloat16
I32 = jnp.int32

HEAD_DIM = 64
N_HEADS = 8
WIDTH = N_HEADS * HEAD_DIM
IDX_DIM = 64
N_IDX_HEADS = 8
TOPK_MAX = 256
ROPE_THETA = 10000.0
RMS_EPS = 1e-6
EXPERT_TOP_K = 8
N_EXPERT_GROUPS = 8
TOPK_GROUPS = 4
ROUTED_SCALE = 2.5

LANES = 128
ROW_TILE = 256
ATT_TILE = 256
PAGES_PER_STEP = 8
MOE_BLOCK_ROWS = 256
SC_WORKERS = 32
SC_WINDOW = 40
SC_INDEX_TILE = 128
VMEM_LIMIT = 56 * 1024 * 1024
INT_MIN = -(2 ** 31)
NEG_INF = float("-inf")


def _cparams(*sem):
    return pltpu.CompilerParams(dimension_semantics=sem, vmem_limit_bytes=VMEM_LIMIT)


def _dot(a, b):
    return jnp.dot(a, b, preferred_element_type=F32)


def _dot_nt(a, b):
    return lax.dot_general(a, b, (((1,), (1,)), ((), ())), preferred_element_type=F32)


def _split_bf16(v):
    hi = v.astype(BF16)
    lo = (v - hi.astype(F32)).astype(BF16)
    return hi, lo


def _rms(v, g):
    return v * lax.rsqrt(jnp.mean(v * v, axis=-1, keepdims=True) + RMS_EPS) * g


def _ada_kernel(c_ref, w_ref, b_ref, o_ref):
    c = c_ref[...]
    a_hi, a_lo = _split_bf16(c * jax.nn.sigmoid(c))
    w_hi, w_lo = _split_bf16(w_ref[...])
    o_ref[...] = _dot(a_hi, w_hi) + _dot(a_hi, w_lo) + _dot(a_lo, w_hi) + b_ref[...]


def _ada(c, w, b):
    m, d = c.shape
    n = w.shape[1]
    tn = 1024 if n % 1024 == 0 else n
    return pl.pallas_call(
        _ada_kernel,
        grid=(n // tn,),
        in_specs=[pl.BlockSpec((m, d), lambda j: (0, 0)),
                  pl.BlockSpec((d, tn), lambda j: (0, j)),
                  pl.BlockSpec((1, tn), lambda j: (0, j))],
        out_specs=pl.BlockSpec((m, tn), lambda j: (0, j)),
        out_shape=jax.ShapeDtypeStruct((m, n), F32),
        compiler_params=_cparams("arbitrary"),
        name="ada_mod",
    )(c, w, b.reshape(1, n))


def _rope_cols(v, cos, sin_signed):
    rows = v.shape[0]
    lane = lax.broadcasted_iota(I32, (rows, LANES), 1)
    first_half = (lane % HEAD_DIM) < (HEAD_DIM // 2)
    outs = []
    for c in range(v.shape[1] // LANES):
        ch = v[:, c * LANES:(c + 1) * LANES]
        partner = jnp.where(first_half, pltpu.roll(ch, LANES - HEAD_DIM // 2, 1), pltpu.roll(ch, HEAD_DIM // 2, 1))
        outs.append(ch * cos + partner * sin_signed)
    return outs[0] if len(outs) == 1 else jnp.concatenate(outs, axis=1)


def _inproj_kernel(x_ref, sc_ref, sh_ref, g_ref, cos_ref, sin_ref,
                   wqa_ref, wkva_ref, wqi_ref, wsm_ref, wqb_ref, wkvb_ref, wg_ref,
                   qa_o, qi_o, qb_o, sm_o, ka_o, kb_o, ki_o, ga_o, gb_o, *state_o):
    d = x_ref.shape[1]
    h = _rms(x_ref[...], g_ref[...]) * (1.0 + sc_ref[0]) + sh_ref[0]
    hb = h.astype(BF16)
    cos = cos_ref[...]
    sin = sin_ref[...]

    qa_o[...] = _rope_cols(_dot(hb, wqa_ref[...]), cos, sin).astype(BF16)
    qi_o[...] = _rope_cols(_dot(hb, wqi_ref[...]), cos, sin).astype(BF16)
    qb_o[...] = _dot(hb, wqb_ref[...]).astype(BF16)

    kva = _dot(hb, wkva_ref[...])
    kva = jnp.concatenate([_rope_cols(kva[:, :WIDTH], cos, sin), kva[:, WIDTH:]], axis=1)
    ka_o[...] = kva[:, :WIDTH].astype(BF16)
    kvb = _dot(hb, wkvb_ref[...])
    kb_o[...] = kvb[:, :WIDTH].astype(BF16)

    sm = _dot(hb, wsm_ref[...])
    lane = lax.broadcasted_iota(I32, sm.shape, 1)
    sm = jnp.where(lane < IDX_DIM, _rope_cols(sm, cos, sin), sm)
    sm_o[...] = sm
    ki_o[...] = sm.astype(BF16)

    g = jax.nn.sigmoid(_dot(hb, wg_ref[...]))
    ga_o[...] = g[:, :d]
    gb_o[...] = g[:, d:]

    if len(state_o) == 2:
        kva_o, kvb_o = state_o
        kva_o[...] = kva
        kvb_o[...] = kvb
    else:
        kvat_o, kvbt_o, kit_o, vat_o, vbt_o = state_o
        kva_t, kvb_t = kva.T, kvb.T
        kvat_o[0] = kva_t
        kvbt_o[0] = kvb_t
        kit_o[0] = sm.T[:IDX_DIM, :]
        vat_o[0, 0] = kva_t[WIDTH:, :].astype(BF16)
        vbt_o[0, 0] = kvb_t[WIDTH:, :].astype(BF16)


def _inproj(x, scale, shift, mod_map, g, cos, sin, tbl_blocks, ws, *, tm, seq_tiles):
    n, d = x.shape
    r = scale.shape[1]
    grid = (n // tm,)
    row = lambda i: (i, 0)
    const = lambda i: (0, 0)
    in_specs = [pl.BlockSpec((tm, d), row),
                pl.BlockSpec((1, r, d), mod_map),
                pl.BlockSpec((1, r, d), mod_map),
                pl.BlockSpec((1, d), const),
                pl.BlockSpec((tm, LANES), lambda i: (i % tbl_blocks, 0)),
                pl.BlockSpec((tm, LANES), lambda i: (i % tbl_blocks, 0))]
    in_specs += [pl.BlockSpec(w.shape, const) for w in ws]
    out_cols = [(WIDTH, BF16), (WIDTH, BF16), (WIDTH, BF16), (LANES, F32),
                (WIDTH, BF16), (WIDTH, BF16), (LANES, BF16), (d, F32), (d, F32)]
    if seq_tiles is None:
        out_cols += [(2 * WIDTH, F32), (2 * WIDTH, F32)]
    out_specs = [pl.BlockSpec((tm, c), row) for c, _ in out_cols]
    out_shape = [jax.ShapeDtypeStruct((n, c), t) for c, t in out_cols]
    if seq_tiles is not None:
        nb = n // (tm * seq_tiles)
        seq_map = lambda i: (i // seq_tiles, 0, i % seq_tiles)
        vt_map = lambda i: (i // seq_tiles, i % seq_tiles, 0, 0)
        out_specs += [pl.BlockSpec((1, 2 * WIDTH, tm), seq_map)] * 2 + [pl.BlockSpec((1, IDX_DIM, tm), seq_map)]
        out_shape += [jax.ShapeDtypeStruct((nb, 2 * WIDTH, tm * seq_tiles), F32)] * 2
        out_shape += [jax.ShapeDtypeStruct((nb, IDX_DIM, tm * seq_tiles), F32)]
        out_specs += [pl.BlockSpec((1, 1, WIDTH, tm), vt_map)] * 2
        out_shape += [jax.ShapeDtypeStruct((nb, seq_tiles, WIDTH, tm), BF16)] * 2
    return pl.pallas_call(
        _inproj_kernel, grid=grid, in_specs=in_specs, out_specs=out_specs, out_shape=out_shape,
        compiler_params=_cparams("arbitrary"), name="in_proj",
    )(x, scale, shift, g.reshape(1, d), cos, sin, *ws)


def _sortable(v):
    b = lax.bitcast_convert_type(v, I32)
    return b ^ ((b >> 31) & jnp.int32(0x7FFFFFFF))


def _select_topk(load_keys, n_chunks, part_sum, final_sum, pos_of_chunk, shape1, part_shape, k_top, idx_bits):
    def count(pred):
        def body(c, acc):
            return acc + part_sum(jnp.where(pred(load_keys(c), c), 1.0, 0.0))
        return final_sum(lax.fori_loop(0, n_chunks, body, jnp.zeros(part_shape, F32), unroll=isinstance(n_chunks, int)))

    def bit_body(bi, t_u):
        cand_u = t_u | jnp.left_shift(jnp.int32(1), 31 - bi)
        cand_s = cand_u ^ jnp.int32(INT_MIN)
        cnt = count(lambda k, c: k >= cand_s)
        return jnp.where(cnt >= k_top, cand_u, t_u)

    t_u = lax.fori_loop(0, 32, bit_body, jnp.zeros(shape1, I32))
    thr = t_u ^ jnp.int32(INT_MIN)
    need = k_top - count(lambda k, c: k > thr)

    def idx_body(bi, cut):
        cand = cut | jnp.left_shift(jnp.int32(1), idx_bits - 1 - bi)
        below = count(lambda k, c: (k == thr) & (pos_of_chunk(c) < cand))
        return jnp.where(below < need, cand, cut)

    n_ge = count(lambda k, c: (k >= thr) & (k != jnp.int32(INT_MIN)))
    split_ties = jnp.max(n_ge) > k_top
    cut = lax.cond(split_ties,
                   lambda: lax.fori_loop(0, idx_bits, idx_body, jnp.zeros(shape1, I32)),
                   lambda: jnp.full(shape1, (1 << idx_bits) - 1, I32))
    return thr, cut


def _dsa_prompt_kernel(qi_ref, qa_ref, sm_ref, ki_ref, ka_ref, vat_ref, o_ref, keys_scr, acc_scr, m_scr, l_scr,
                       *, k_top, idx_bits):
    t = qi_ref.shape[0]
    i = pl.program_id(1)
    n_chunks = i + 1
    q_pos = i * t + lax.broadcasted_iota(I32, (t, t), 1)
    k_iota = lax.broadcasted_iota(I32, (t, t), 0)

    qit = qi_ref[...].astype(F32).T.astype(BF16)
    smt = sm_ref[...].T
    idx_scale = IDX_DIM ** -0.5 * N_IDX_HEADS ** -0.5

    def score_chunk(c, carry):
        off = pl.multiple_of(c * t, t)
        kc = ki_ref[pl.ds(off, t), :][:, :IDX_DIM]
        acc = jnp.zeros((t, t), F32)
        for h in range(N_IDX_HEADS):
            s = _dot(kc, qit[h * IDX_DIM:(h + 1) * IDX_DIM, :])
            acc = acc + jnp.maximum(s, 0.0) * smt[IDX_DIM + h:IDX_DIM + h + 1, :]
        key = _sortable(acc * idx_scale)
        keys_scr[pl.ds(off, t), :] = jnp.where(c * t + k_iota <= q_pos, key, jnp.int32(INT_MIN))
        return carry

    lax.fori_loop(0, n_chunks, score_chunk, 0)

    def load_keys(c):
        return keys_scr[pl.ds(pl.multiple_of(c * t, t), t), :]

    thr, cut = _select_topk(load_keys, n_chunks, lambda x: x.reshape(t // 8, 8, t).sum(axis=0),
                            lambda p: jnp.sum(p, axis=0, keepdims=True), lambda c: c * t + k_iota,
                            (1, t), (8, t), k_top, idx_bits)

    qat = (qa_ref[...].astype(F32) * HEAD_DIM ** -0.5).T.astype(BF16)
    acc_scr[...] = jnp.zeros_like(acc_scr)
    m_scr[...] = jnp.full_like(m_scr, NEG_INF)
    l_scr[...] = jnp.zeros_like(l_scr)

    def att_chunk(c, carry):
        off = pl.multiple_of(c * t, t)
        key = keys_scr[pl.ds(off, t), :]
        sel = ((key > thr) | ((key == thr) & (c * t + k_iota <= cut))) & (key != jnp.int32(INT_MIN))
        k = ka_ref[pl.ds(off, t), :]
        heads = [slice(h * HEAD_DIM, (h + 1) * HEAD_DIM) for h in range(N_HEADS)]
        logits = [jnp.where(sel, _dot(k[:, hs], qat[hs, :]), NEG_INF) for hs in heads]
        m_old = m_scr[...]
        m_new = jnp.maximum(m_old, jnp.concatenate([jnp.max(lt, axis=0, keepdims=True) for lt in logits], axis=0))
        m_safe = jnp.where(m_new == NEG_INF, 0.0, m_new)
        alpha = jnp.exp(m_old - m_safe)
        p = [jnp.exp(lt - m_safe[h:h + 1, :]) for h, lt in enumerate(logits)]
        l_scr[...] = alpha * l_scr[...] + jnp.concatenate([jnp.sum(ph, axis=0, keepdims=True) for ph in p], axis=0)
        m_scr[...] = m_new
        pv = [_dot(vat_ref[c, hs, :], ph.astype(BF16)) for hs, ph in zip(heads, p)]
        for h, hs in enumerate(heads):
            acc_scr[hs, :] = alpha[h:h + 1, :] * acc_scr[hs, :] + pv[h]
        return carry

    lax.fori_loop(0, n_chunks, att_chunk, 0)

    for h in range(N_HEADS):
        hs = slice(h * HEAD_DIM, (h + 1) * HEAD_DIM)
        acc_scr[hs, :] = acc_scr[hs, :] * (1.0 / l_scr[h:h + 1, :])
    o_ref[...] = acc_scr[...].T.astype(BF16)


def _dsa_prompt(qi, qa, sm, ki, ka, vat, *, b, s, k_top):
    t = ATT_TILE
    nq = s // t
    qmap = lambda bi, i: (bi * nq + i, 0)
    kmap = lambda bi, i: (bi, 0)
    return pl.pallas_call(
        functools.partial(_dsa_prompt_kernel, k_top=k_top, idx_bits=max(1, (s - 1).bit_length())),
        grid=(b, nq),
        in_specs=[pl.BlockSpec((t, WIDTH), qmap), pl.BlockSpec((t, WIDTH), qmap), pl.BlockSpec((t, LANES), qmap),
                  pl.BlockSpec((s, LANES), kmap), pl.BlockSpec((s, WIDTH), kmap),
                  pl.BlockSpec((None, nq, WIDTH, t), lambda bi, i: (bi, 0, 0, 0))],
        out_specs=pl.BlockSpec((t, WIDTH), qmap),
        out_shape=jax.ShapeDtypeStruct((b * s, WIDTH), BF16),
        scratch_shapes=[pltpu.VMEM((s, t), I32), pltpu.VMEM((WIDTH, t), F32),
                        pltpu.VMEM((N_HEADS, t), F32), pltpu.VMEM((N_HEADS, t), F32)],
        compiler_params=_cparams("arbitrary", "arbitrary"),
        name="dsa_prompt",
    )(qi, qa, sm, ki, ka, vat)


def _log_sigmoid_pair(z):
    tail = jnp.log(1.0 + jnp.exp(-jnp.abs(z)))
    return jnp.minimum(z, 0.0) - tail, jnp.minimum(-z, 0.0) - tail


def _sb_prompt_kernel(qb_ref, kb_ref, vbt_ref, o_ref, acc_scr, carry_scr):
    t = qb_ref.shape[0]
    i = pl.program_id(1)
    q_pos = i * t + lax.broadcasted_iota(I32, (t, t), 1)
    k_iota = lax.broadcasted_iota(I32, (t, t), 0)
    later = (lax.broadcasted_iota(I32, (t, t), 1) > k_iota).astype(BF16)
    qt = (qb_ref[...].astype(F32) * HEAD_DIM ** -0.5).T.astype(BF16)
    acc_scr[...] = jnp.zeros_like(acc_scr)
    carry_scr[...] = jnp.zeros_like(carry_scr)

    heads = [slice(h * HEAD_DIM, (h + 1) * HEAD_DIM) for h in range(N_HEADS)]

    def chunk(c, mask):
        off = pl.multiple_of(c * t, t)
        k = kb_ref[pl.ds(off, t), :]
        z = [_dot(k[:, hs], qt[hs, :]) for hs in heads]
        pairs = [_log_sigmoid_pair(zh) for zh in z]
        log_keep = [lk if mask is None else jnp.where(mask, lk, 0.0) for _, lk in pairs]
        split = [_split_bf16(lk) for lk in log_keep]
        suffix = [_dot(later, hi) + _dot(later, lo) + carry_scr[h:h + 1, :] for h, (hi, lo) in enumerate(split)]
        w = [jnp.exp(lb + sf) for (lb, _), sf in zip(pairs, suffix)]
        if mask is not None:
            w = [jnp.where(mask, wh, 0.0) for wh in w]
        pv = [_dot(vbt_ref[c, hs, :], wh.astype(BF16)) for hs, wh in zip(heads, w)]
        for h, hs in enumerate(heads):
            acc_scr[hs, :] = acc_scr[hs, :] + pv[h]
            carry_scr[h:h + 1, :] = carry_scr[h:h + 1, :] + jnp.sum(log_keep[h], axis=0, keepdims=True)

    chunk(i, i * t + k_iota < q_pos)

    def earlier_chunk(j, carry):
        chunk(i - 1 - j, None)
        return carry

    lax.fori_loop(0, i, earlier_chunk, 0)
    o_ref[...] = acc_scr[...].T.astype(BF16)


def _sb_prompt(qb, kb, vbt, *, b, s):
    t = ATT_TILE
    nq = s // t
    qmap = lambda bi, i: (bi * nq + i, 0)
    return pl.pallas_call(
        _sb_prompt_kernel,
        grid=(b, nq),
        in_specs=[pl.BlockSpec((t, WIDTH), qmap), pl.BlockSpec((s, WIDTH), lambda bi, i: (bi, 0)),
                  pl.BlockSpec((None, nq, WIDTH, t), lambda bi, i: (bi, 0, 0, 0))],
        out_specs=pl.BlockSpec((t, WIDTH), qmap),
        out_shape=jax.ShapeDtypeStruct((b * s, WIDTH), BF16),
        scratch_shapes=[pltpu.VMEM((WIDTH, t), F32), pltpu.VMEM((N_HEADS, t), F32)],
        compiler_params=_cparams("arbitrary", "arbitrary"),
        name="sb_prompt",
    )(qb, kb, vbt)


def _rows_by_head(q):
    return jnp.concatenate([q[:, h * HEAD_DIM:(h + 1) * HEAD_DIM] for h in range(N_HEADS)], axis=0)


def _block_diag_rows(q):
    t = q.shape[0]
    rows = jnp.concatenate([q] * N_HEADS, axis=0)
    r = lax.broadcasted_iota(I32, rows.shape, 0) // t
    c = lax.broadcasted_iota(I32, rows.shape, 1) // HEAD_DIM
    return jnp.where(r == c, rows, 0.0)


def _diag_heads(acc, t):
    return jnp.concatenate([acc[h * t:(h + 1) * t, h * HEAD_DIM:(h + 1) * HEAD_DIM] for h in range(N_HEADS)], axis=1)


def _pad_rows(v, rows):
    return jnp.concatenate([v, jnp.zeros((rows - v.shape[0], v.shape[1]), v.dtype)], axis=0)


def _pages_kv_t(pages):
    k_t = jnp.concatenate([pg[:WIDTH, :] for pg in pages], axis=1).astype(BF16)
    v_t = jnp.concatenate([pg[WIDTH:, :] for pg in pages], axis=1).astype(BF16)
    return k_t, v_t


def _new_kv_t(kv_new, page):
    kv_t = _pad_rows(kv_new, page).T.astype(BF16)
    return kv_t[:WIDTH, :], kv_t[WIDTH:, :]


def _dsa_sample_select_kernel(pt_ref, *refs, k_top, idx_bits, n_steps):
    pages = refs[:PAGES_PER_STEP]
    qi_ref, sm_ref, bias_o, keys_scr = refs[PAGES_PER_STEP:]
    j = pl.program_id(1)
    t = qi_ref.shape[0]
    page = pages[0].shape[1]
    cw = PAGES_PER_STEP * page
    idx_scale = IDX_DIM ** -0.5 * N_IDX_HEADS ** -0.5
    q_rows = _rows_by_head(qi_ref[...].astype(F32)).astype(BF16)
    sm = sm_ref[...]
    w_cols = [jnp.broadcast_to(sm[:, IDX_DIM + h:IDX_DIM + h + 1], (t, page)) for h in range(N_IDX_HEADS)]

    def scores(k_t):
        s = _dot(q_rows, k_t)
        tiles = []
        for c in range(k_t.shape[1] // page):
            acc = jnp.zeros((t, page), F32)
            for h in range(N_IDX_HEADS):
                acc = acc + jnp.maximum(s[h * t:(h + 1) * t, c * page:(c + 1) * page], 0.0) * w_cols[h]
            tiles.append(acc * idx_scale)
        return tiles[0] if len(tiles) == 1 else jnp.concatenate(tiles, axis=1)

    keys_scr[j] = _sortable(scores(jnp.concatenate([pg[...] for pg in pages], axis=1).astype(BF16)))

    @pl.when(j == n_steps - 1)
    def _():
        k_new = _pad_rows(sm[:, :IDX_DIM], page).T.astype(BF16)
        key = _sortable(scores(k_new))
        u = lax.broadcasted_iota(I32, (t, page), 1)
        q = lax.broadcasted_iota(I32, (t, page), 0)
        keys_scr[n_steps, :, :page] = jnp.where(u <= q, key, jnp.int32(INT_MIN))
        keys_scr[n_steps, :, page:] = jnp.full((t, cw - page), INT_MIN, I32)

        lane = lax.broadcasted_iota(I32, (t, cw), 1)
        part = lambda x: functools.reduce(lambda a, b2: a + b2, [x[:, k * page:(k + 1) * page] for k in range(PAGES_PER_STEP)])
        final = lambda p: jnp.sum(p, axis=1, keepdims=True)
        thr, cut = _select_topk(lambda c: keys_scr[c], n_steps + 1, part, final, lambda c: c * cw + lane,
                                (t, 1), (t, page), k_top, idx_bits)

        for c in range(n_steps + 1):
            key = keys_scr[c]
            sel = ((key > thr) | ((key == thr) & (c * cw + lane <= cut))) & (key != jnp.int32(INT_MIN))
            bias_o[c] = jnp.where(sel, 0.0, NEG_INF)


def _dsa_sample_attend_kernel(pt_ref, *refs, n_steps):
    pages = refs[:PAGES_PER_STEP]
    qa_ref, kv_new_ref, bias_ref, bias_new_ref, o_ref, acc_scr, m_scr, l_scr = refs[PAGES_PER_STEP:]
    j = pl.program_id(1)
    t = qa_ref.shape[0]
    page = pages[0].shape[1]
    q_bd = _block_diag_rows(qa_ref[...].astype(F32) * HEAD_DIM ** -0.5).astype(BF16)

    @pl.when(j == 0)
    def _():
        acc_scr[...] = jnp.zeros_like(acc_scr)
        m_scr[...] = jnp.full_like(m_scr, NEG_INF)
        l_scr[...] = jnp.zeros_like(l_scr)

    def update(k_t, v_t, bias):
        logits = _dot(q_bd, k_t) + jnp.concatenate([bias] * N_HEADS, axis=0)
        m_old = m_scr[...]
        m_new = jnp.maximum(m_old, jnp.max(logits, axis=1, keepdims=True))
        m_safe = jnp.where(m_new == NEG_INF, 0.0, m_new)
        alpha = jnp.exp(m_old - m_safe)
        p = jnp.exp(logits - m_safe)
        l_scr[...] = alpha * l_scr[...] + jnp.sum(p, axis=1, keepdims=True)
        m_scr[...] = m_new
        acc_scr[...] = alpha * acc_scr[...] + _dot_nt(p.astype(BF16), v_t)

    update(*_pages_kv_t(pages), bias_ref[...])

    @pl.when(j == n_steps - 1)
    def _():
        update(*_new_kv_t(kv_new_ref[...], page), bias_new_ref[:, :page])
        o_ref[...] = _diag_heads(acc_scr[...] * (1.0 / l_scr[...]), t).astype(BF16)


def _page_specs(cache, layer, n_pages, reverse=False):
    page, width = cache.shape[2:]

    def spec(r):
        if reverse:
            return pl.BlockSpec((None, None, page, width),
                                lambda b, j, pt: (layer, pt[b, n_pages - 1 - (j * PAGES_PER_STEP + r)], 0, 0))
        return pl.BlockSpec((None, None, page, width), lambda b, j, pt: (layer, pt[b, j * PAGES_PER_STEP + r], 0, 0))
    return [spec(r) for r in range(PAGES_PER_STEP)]


def _dsa_sample(qi, qa, sm, kva_new, cache_idx, cache_kv, page_table, layer, *, k_top):
    db, t, _ = qi.shape
    n_pages = page_table.shape[1]
    n_steps = n_pages // PAGES_PER_STEP
    page = cache_idx.shape[3]
    cw = PAGES_PER_STEP * page
    per_b = lambda width: pl.BlockSpec((None, t, width), lambda b, j, pt: (b, 0, 0))
    n_keys = n_pages * page + t
    bias = pl.pallas_call(
        functools.partial(_dsa_sample_select_kernel, k_top=k_top, idx_bits=max(1, (n_keys - 1).bit_length()), n_steps=n_steps),
        grid_spec=pltpu.PrefetchScalarGridSpec(
            num_scalar_prefetch=1, grid=(db, n_steps),
            in_specs=_page_specs(cache_idx, layer, n_pages) + [per_b(WIDTH), per_b(LANES)],
            out_specs=pl.BlockSpec((None, n_steps + 1, t, cw), lambda b, j, pt: (b, 0, 0, 0)),
            scratch_shapes=[pltpu.VMEM((n_steps + 1, t, cw), I32)]),
        out_shape=jax.ShapeDtypeStruct((db, n_steps + 1, t, cw), F32),
        compiler_params=_cparams("arbitrary", "arbitrary"),
        name="dsa_sample_select",
    )(page_table, *([cache_idx] * PAGES_PER_STEP), qi, sm)
    return pl.pallas_call(
        functools.partial(_dsa_sample_attend_kernel, n_steps=n_steps),
        grid_spec=pltpu.PrefetchScalarGridSpec(
            num_scalar_prefetch=1, grid=(db, n_steps),
            in_specs=_page_specs(cache_kv, layer, n_pages) + [
                per_b(WIDTH), per_b(2 * WIDTH),
                pl.BlockSpec((None, None, t, cw), lambda b, j, pt: (b, j, 0, 0)),
                pl.BlockSpec((None, None, t, cw), lambda b, j, pt: (b, n_steps, 0, 0))],
            out_specs=per_b(WIDTH),
            scratch_shapes=[pltpu.VMEM((N_HEADS * t, WIDTH), F32), pltpu.VMEM((N_HEADS * t, 1), F32),
                            pltpu.VMEM((N_HEADS * t, 1), F32)]),
        out_shape=jax.ShapeDtypeStruct((db, t, WIDTH), BF16),
        compiler_params=_cparams("arbitrary", "arbitrary"),
        name="dsa_sample_attend",
    )(page_table, *([cache_kv] * PAGES_PER_STEP), qa, kva_new, bias, bias)


def _suffix_scan_lanes(x):
    lane = lax.broadcasted_iota(I32, x.shape, 1)
    sh = 1
    while sh < LANES:
        x = x + jnp.where(lane + sh < LANES, pltpu.roll(x, LANES - sh, 1), 0.0)
        sh *= 2
    return x


def _sb_sample_kernel(pt_ref, *refs, n_steps):
    pages = refs[:PAGES_PER_STEP]
    qb_ref, kv_new_ref, o_ref, acc_scr, carry_scr = refs[PAGES_PER_STEP:]
    j = pl.program_id(1)
    t = qb_ref.shape[0]
    page = pages[0].shape[1]
    q_bd = _block_diag_rows(qb_ref[...].astype(F32) * HEAD_DIM ** -0.5).astype(BF16)

    def weights(z, mask):
        outs = []
        carry = carry_scr[...]
        for k in range(z.shape[1] // page):
            zt = z[:, k * page:(k + 1) * page]
            log_beta, log_keep = _log_sigmoid_pair(zt)
            if mask is not None:
                log_keep = jnp.where(mask, log_keep, 0.0)
            incl = _suffix_scan_lanes(log_keep)
            w = jnp.exp(log_beta + (incl - log_keep) + carry)
            if mask is not None:
                w = jnp.where(mask, w, 0.0)
            outs.append(w)
            carry = carry + incl[:, :1]
        carry_scr[...] = carry
        return outs[0] if len(outs) == 1 else jnp.concatenate(outs, axis=1)

    @pl.when(j == 0)
    def _():
        carry_scr[...] = jnp.zeros_like(carry_scr)
        k_t, v_t = _new_kv_t(kv_new_ref[...], page)
        u = lax.broadcasted_iota(I32, (N_HEADS * t, page), 1)
        q = lax.broadcasted_iota(I32, (N_HEADS * t, page), 0) % t
        w = weights(_dot(q_bd, k_t), u < q)
        acc_scr[...] = _dot_nt(w.astype(BF16), v_t)

    k_t, v_t = _pages_kv_t(pages)
    w = weights(_dot(q_bd, k_t), None)
    acc_scr[...] = acc_scr[...] + _dot_nt(w.astype(BF16), v_t)

    @pl.when(j == n_steps - 1)
    def _():
        o_ref[...] = _diag_heads(acc_scr[...], t).astype(BF16)


def _sb_sample(qb, kvb_new, cache_kv, page_table, layer):
    db, t, _ = qb.shape
    n_pages = page_table.shape[1]
    n_steps = n_pages // PAGES_PER_STEP
    per_b = lambda width: pl.BlockSpec((None, t, width), lambda b, j, pt: (b, 0, 0))
    return pl.pallas_call(
        functools.partial(_sb_sample_kernel, n_steps=n_steps),
        grid_spec=pltpu.PrefetchScalarGridSpec(
            num_scalar_prefetch=1, grid=(db, n_steps),
            in_specs=_page_specs(cache_kv, layer, n_pages, reverse=True) + [per_b(WIDTH), per_b(2 * WIDTH)],
            out_specs=per_b(WIDTH),
            scratch_shapes=[pltpu.VMEM((N_HEADS * t, WIDTH), F32), pltpu.VMEM((N_HEADS * t, 1), F32)]),
        out_shape=jax.ShapeDtypeStruct((db, t, WIDTH), BF16),
        compiler_params=_cparams("arbitrary", "arbitrary"),
        name="sb_sample",
    )(page_table, *([cache_kv] * PAGES_PER_STEP), qb, kvb_new)


def _outproj_kernel(oa_ref, ob_ref, ga_ref, gb_ref, x_ref, gate_ref, sc_ref, sh_ref, gpost_ref, gpre_ref,
                    wpa_ref, wpb_ref, wout_ref, wrh_ref, wrl_ref, x1_o, h2_o, lg_o):
    mix = ga_ref[...] * _dot(oa_ref[...], wpa_ref[...]) + gb_ref[...] * _dot(ob_ref[...], wpb_ref[...])
    mo = _dot(mix.astype(BF16), wout_ref[...])
    x1 = x_ref[...] + gate_ref[0] * _rms(mo, gpost_ref[...])
    x1_o[...] = x1
    h2 = _rms(x1, gpre_ref[...]) * (1.0 + sc_ref[0]) + sh_ref[0]
    h_hi, h_lo = _split_bf16(h2)
    h2_o[...] = h2
    lg_o[...] = _dot_nt(wrh_ref[...], h_hi) + _dot_nt(wrh_ref[...], h_lo) + _dot_nt(wrl_ref[...], h_hi)


def _outproj(oa, ob, ga, gb, x, gate, scale, shift, mod_map, gpost, gpre, ws, *, tm):
    n, d = x.shape
    r = gate.shape[1]
    e = ws[3].shape[0]
    row = lambda i: (i, 0)
    const = lambda i: (0, 0)
    in_specs = [pl.BlockSpec((tm, WIDTH), row), pl.BlockSpec((tm, WIDTH), row),
                pl.BlockSpec((tm, d), row), pl.BlockSpec((tm, d), row), pl.BlockSpec((tm, d), row),
                pl.BlockSpec((1, r, d), mod_map), pl.BlockSpec((1, r, d), mod_map), pl.BlockSpec((1, r, d), mod_map),
                pl.BlockSpec((1, d), const), pl.BlockSpec((1, d), const)]
    in_specs += [pl.BlockSpec(w.shape, const) for w in ws]
    return pl.pallas_call(
        _outproj_kernel, grid=(n // tm,), in_specs=in_specs,
        out_specs=[pl.BlockSpec((tm, d), row), pl.BlockSpec((tm, d), row), pl.BlockSpec((e, tm), lambda i: (0, i))],
        out_shape=[jax.ShapeDtypeStruct((n, d), F32), jax.ShapeDtypeStruct((n, d), F32),
                   jax.ShapeDtypeStruct((e, n), F32)],
        compiler_params=_cparams("arbitrary"), name="out_proj",
    )(oa, ob, ga, gb, x, gate, scale, shift, gpost.reshape(1, d), gpre.reshape(1, d), *ws)


def _route_kernel(lg_ref, b_ref, eidx_o, gw_o, rank_o, count_o, count_scr):
    @pl.when(pl.program_id(0) == 0)
    def _():
        count_scr[...] = jnp.zeros_like(count_scr)

    scores = jax.nn.sigmoid(lg_ref[...])
    biased = scores + b_ref[...]
    e, tn = scores.shape
    gsz = e // N_EXPERT_GROUPS
    group_score = []
    for g in range(N_EXPERT_GROUPS):
        xg = biased[g * gsz:(g + 1) * gsz, :]
        m1 = jnp.max(xg, axis=0, keepdims=True)
        n1 = jnp.sum(jnp.where(xg == m1, 1.0, 0.0), axis=0, keepdims=True)
        m2 = jnp.max(jnp.where(xg < m1, xg, NEG_INF), axis=0, keepdims=True)
        group_score.append(m1 + jnp.where(n1 >= 2.0, m1, m2))
    cur = []
    for g in range(N_EXPERT_GROUPS):
        rank = jnp.zeros((1, tn), F32)
        for g2 in range(N_EXPERT_GROUPS):
            if g2 == g:
                continue
            ahead = (group_score[g2] > group_score[g]) | ((group_score[g2] == group_score[g]) & (g2 < g))
            rank = rank + jnp.where(ahead, 1.0, 0.0)
        cur.append(jnp.where(rank < TOPK_GROUPS, biased[g * gsz:(g + 1) * gsz, :], NEG_INF))
    cur = jnp.concatenate(cur, axis=0)
    row = lax.broadcasted_iota(I32, (e, tn), 0).astype(F32)
    picked, hits = [], []
    for k in range(EXPERT_TOP_K):
        m = jnp.max(cur, axis=0, keepdims=True)
        idx = jnp.min(jnp.where(cur == m, row, float(e)), axis=0, keepdims=True)
        hit = row == idx
        hits.append(hit)
        eidx_o[k:k + 1, :] = idx.astype(I32)
        picked.append(jnp.sum(jnp.where(hit, scores, 0.0), axis=0, keepdims=True))
        cur = jnp.where(hit, NEG_INF, cur)
    total = functools.reduce(lambda a, b2: a + b2, picked)
    for k in range(EXPERT_TOP_K):
        gw_o[k:k + 1, :] = picked[k] / total * ROUTED_SCALE

    onehot = functools.reduce(lambda a, b2: a | b2, hits)
    earlier = lax.broadcasted_iota(I32, (tn, tn), 0) < lax.broadcasted_iota(I32, (tn, tn), 1)
    before = _dot(jnp.where(onehot, 1.0, 0.0).astype(BF16), earlier.astype(BF16)) + count_scr[...]
    for k in range(EXPERT_TOP_K):
        rank_o[k:k + 1, :] = jnp.sum(jnp.where(hits[k], before, 0.0), axis=0, keepdims=True).astype(I32)
    count_scr[...] = count_scr[...] + jnp.sum(jnp.where(onehot, 1.0, 0.0), axis=1, keepdims=True)
    count_o[...] = count_scr[...]


def _route(logits_t, b_router, *, tn):
    e, n = logits_t.shape
    tok = pl.BlockSpec((EXPERT_TOP_K, tn), lambda i: (0, i))
    return pl.pallas_call(
        _route_kernel, grid=(n // tn,),
        in_specs=[pl.BlockSpec((e, tn), lambda i: (0, i)), pl.BlockSpec((e, 1), lambda i: (0, 0))],
        out_specs=[tok, tok, tok, pl.BlockSpec((e, 1), lambda i: (0, 0))],
        out_shape=[jax.ShapeDtypeStruct((EXPERT_TOP_K, n), I32), jax.ShapeDtypeStruct((EXPERT_TOP_K, n), F32),
                   jax.ShapeDtypeStruct((EXPERT_TOP_K, n), I32), jax.ShapeDtypeStruct((e, 1), F32)],
        scratch_shapes=[pltpu.VMEM((e, 1), F32)],
        compiler_params=_cparams("arbitrary"), name="moe_route",
    )(logits_t, b_router.reshape(e, 1))


def _moe_dest_kernel(eidx_ref, rank_ref, start_ref, dest_o):
    e = start_ref.shape[0]
    tn = eidx_ref.shape[1]
    row = lax.broadcasted_iota(I32, (e, tn), 0)
    start = start_ref[...]
    for k in range(EXPERT_TOP_K):
        base = jnp.sum(jnp.where(row == eidx_ref[k:k + 1, :], start, 0.0), axis=0, keepdims=True)
        dest_o[k:k + 1, :] = base.astype(I32) + rank_ref[k:k + 1, :]


def _moe_dest(eidx, rank, pad_start, *, tn):
    k, n = eidx.shape
    e = pad_start.shape[0]
    tok = pl.BlockSpec((k, tn), lambda i: (0, i))
    return pl.pallas_call(
        _moe_dest_kernel, grid=(n // tn,),
        in_specs=[tok, tok, pl.BlockSpec((e, 1), lambda i: (0, 0))],
        out_specs=tok, out_shape=jax.ShapeDtypeStruct((k, n), I32),
        compiler_params=_cparams("arbitrary"), name="moe_dest",
    )(eidx, rank, pad_start)


def _experts_kernel(be_ref, bv_ref, x_ref, wgu_ref, wdn_ref, y_o, wgu_scr, wdn_scr):
    i = pl.program_id(0)
    br = x_ref.shape[0]
    de = wdn_ref.shape[0]
    valid = bv_ref[i]

    @pl.when((i == 0) | (be_ref[i] != be_ref[jnp.maximum(i - 1, 0)]))
    def _():
        wgu_scr[...] = wgu_ref[...].astype(BF16)
        wdn_scr[...] = wdn_ref[...].astype(BF16)

    @pl.when(valid > 0)
    def _():
        live = lax.broadcasted_iota(I32, (br, 1), 0) < valid
        x = jnp.where(live, x_ref[...], 0.0).astype(BF16)
        gu = _dot(x, wgu_scr[...])
        gate = gu[:, :de]
        act = gate * jax.nn.sigmoid(gate) * gu[:, de:]
        y_o[...] = _dot(act.astype(BF16), wdn_scr[...])

    @pl.when(valid == 0)
    def _():
        y_o[...] = jnp.zeros_like(y_o)


def _experts(x_sorted, block_expert, block_valid, w_gu, w_dn, layer):
    rows, d = x_sorted.shape
    de = w_dn.shape[2]
    br = MOE_BLOCK_ROWS
    return pl.pallas_call(
        _experts_kernel,
        grid_spec=pltpu.PrefetchScalarGridSpec(
            num_scalar_prefetch=2, grid=(rows // br,),
            in_specs=[pl.BlockSpec((br, d), lambda i, be, bv: (i, 0)),
                      pl.BlockSpec((None, None, d, 2 * de), lambda i, be, bv: (layer, be[i], 0, 0)),
                      pl.BlockSpec((None, None, de, d), lambda i, be, bv: (layer, be[i], 0, 0))],
            out_specs=pl.BlockSpec((br, d), lambda i, be, bv: (i, 0)),
            scratch_shapes=[pltpu.VMEM((d, 2 * de), BF16), pltpu.VMEM((de, d), BF16)]),
        out_shape=jax.ShapeDtypeStruct((rows, d), F32),
        compiler_params=_cparams("arbitrary"), name="moe_experts",
    )(block_expert, block_valid, x_sorted, w_gu, w_dn)


def _ffn_out_kernel(h2_ref, yg_ref, gw_ref, x1_ref, gate_ref, gpost_ref, wgu_ref, wdn_ref, y_o):
    ds = wdn_ref.shape[0]
    gu = _dot(h2_ref[...].astype(BF16), wgu_ref[...])
    gate = gu[:, :ds]
    act = gate * jax.nn.sigmoid(gate) * gu[:, ds:]
    f = _dot(act.astype(BF16), wdn_ref[...])
    gw = gw_ref[...]
    for k in range(EXPERT_TOP_K):
        f = f + yg_ref[k] * gw[:, k:k + 1]
    y_o[...] = x1_ref[...] + gate_ref[0] * _rms(f, gpost_ref[...])


def _ffn_out(h2, y_gathered, gw_t, x1, gate, mod_map, gpost, wgu, wdn, *, tm, row0=0):
    n, d = x1.shape
    r = gate.shape[1]
    k = y_gathered.shape[0]
    blk0 = row0 // tm
    row = lambda i: (i, 0)
    const = lambda i: (0, 0)
    return pl.pallas_call(
        _ffn_out_kernel, grid=(n // tm,),
        in_specs=[pl.BlockSpec((tm, d), row), pl.BlockSpec((k, tm, d), lambda i: (0, blk0 + i, 0)),
                  pl.BlockSpec((tm, k), lambda i: (blk0 + i, 0)), pl.BlockSpec((tm, d), row),
                  pl.BlockSpec((1, r, d), mod_map), pl.BlockSpec((1, d), const),
                  pl.BlockSpec(wgu.shape, const), pl.BlockSpec(wdn.shape, const)],
        out_specs=pl.BlockSpec((tm, d), row),
        out_shape=jax.ShapeDtypeStruct((n, d), F32),
        compiler_params=_cparams("arbitrary"), name="ffn_out",
    )(h2, y_gathered, gw_t, x1, gate, gpost.reshape(1, d), wgu, wdn)


def _sc_index_rows(idx):
    steps = idx.shape[0] // SC_WINDOW
    assert idx.shape[0] % SC_WINDOW == 0 and steps % SC_WORKERS == 0
    return jnp.pad(idx.reshape(steps, SC_WINDOW), ((0, 0), (0, SC_INDEX_TILE - SC_WINDOW)))


def _sc_scatter_rows(src, dest, rows_out):
    n, d = src.shape
    k = dest.shape[0]
    steps = n // SC_WINDOW
    index_rows = _sc_index_rows(dest.reshape(-1))
    mesh = plsc.VectorSubcoreMesh(core_axis_name="core", subcore_axis_name="subcore")

    @pl.kernel(out_type=jax.ShapeDtypeStruct((rows_out, d), src.dtype), mesh=mesh, scratch_types=[])
    def scatter(src_hbm, idx_hbm, out_hbm):
        def step(src_vmem, idx_vmem):
            pltpu.sync_copy(src_vmem, out_hbm.at[idx_vmem.at[0, pl.ds(0, SC_WINDOW)]])

        pltpu.emit_pipeline(
            step, grid=(steps, k),
            in_specs=[pl.BlockSpec((SC_WINDOW, d), lambda i, kk: (i, 0)),
                      pl.BlockSpec((1, SC_INDEX_TILE), lambda i, kk: (kk * steps + i, 0))],
            out_specs=[],
            core_axis_name=("core", "subcore"),
            dimension_semantics=(pltpu.PARALLEL, pltpu.ARBITRARY),
        )(src_hbm, idx_hbm)

    return scatter(src, index_rows)


def _sc_gather_rows(src, idx):
    m = idx.shape[0]
    d = src.shape[1]
    index_rows = _sc_index_rows(idx)
    mesh = plsc.VectorSubcoreMesh(core_axis_name="core", subcore_axis_name="subcore")

    @pl.kernel(out_type=jax.ShapeDtypeStruct((m, d), src.dtype), mesh=mesh, scratch_types=[])
    def gather(src_hbm, idx_hbm, out_hbm):
        def step(idx_vmem, out_vmem):
            pltpu.sync_copy(src_hbm.at[idx_vmem.at[0, pl.ds(0, SC_WINDOW)]], out_vmem)

        pltpu.emit_pipeline(
            step, grid=(m // SC_WINDOW,),
            in_specs=[pl.BlockSpec((1, SC_INDEX_TILE), lambda i: (i, 0))],
            out_specs=[pl.BlockSpec((SC_WINDOW, d), lambda i: (i, 0))],
            core_axis_name=("core", "subcore"),
            dimension_semantics=(pltpu.PARALLEL,),
        )(idx_hbm, out_hbm)

    return gather(src, index_rows)


def _rope_tables(pos):
    half = HEAD_DIM // 2
    inv_freq = ROPE_THETA ** (-jnp.arange(half, dtype=F32) / half)
    ang = pos.astype(F32)[:, None] * inv_freq[None, :]
    cos, sin = jnp.cos(ang), jnp.sin(ang)
    reps = LANES // HEAD_DIM
    return (jnp.tile(jnp.concatenate([cos, cos], axis=1), (1, reps)),
            jnp.tile(jnp.concatenate([-sin, sin], axis=1), (1, reps)))


def _expert_layout(counts, n_blocks, br):
    n_experts = counts.shape[0]
    count = counts[:, 0].astype(I32)
    padded = (count + br - 1) // br * br
    pad_end = jnp.cumsum(padded)
    pad_start = pad_end - padded
    block_start = jnp.arange(n_blocks, dtype=I32) * br
    block_expert = jnp.minimum(jnp.sum((pad_end[None, :] <= block_start[:, None]).astype(I32), axis=1), n_experts - 1)
    in_expert = (jnp.arange(n_experts, dtype=I32)[None, :] == block_expert[:, None]) & (block_start[:, None] < pad_end[None, :])
    live_end = jnp.sum(jnp.where(in_expert, (pad_start + count)[None, :], 0), axis=1)
    block_valid = jnp.clip(live_end - block_start, 0, br).astype(I32)
    return pad_start.astype(F32)[:, None], block_expert, block_valid


def kernel(x_prompt, x_sample, c_prompt, c_sample, cache_a_kv, cache_a_idx, cache_b_kv, page_table, w_ada, b_ada,
           g_pre_mix, g_post_mix, g_pre_ffn, g_post_ffn, w_in, w_proj_a, w_proj_b, w_out, w_router, b_router,
           w_exp_gu, w_exp_down, w_sh_gu, w_sh_down):
    b, s, d = x_prompt.shape
    db, t, _ = x_sample.shape
    n_p, n_s = b * s, db * t
    depth = w_ada.shape[0]
    n_layers, n_phys, page = cache_a_kv.shape[:3]
    past = page_table.shape[1] * page
    n_experts = w_router.shape[2]
    tm_p = ROW_TILE
    tm_s = min(ROW_TILE, n_s)
    assert s % ATT_TILE == 0 and ATT_TILE == tm_p and n_s % tm_s == 0 and (n_p + n_s) % tm_p == 0
    assert page_table.shape[1] % PAGES_PER_STEP == 0 and t <= page

    cos_p, sin_p = _rope_tables(jnp.arange(s, dtype=jnp.int32))
    cos_s, sin_s = _rope_tables(past + jnp.arange(t, dtype=jnp.int32))
    cos_s, sin_s = jnp.tile(cos_s, (db, 1)), jnp.tile(sin_s, (db, 1))
    cache_a = cache_a_kv.transpose(0, 1, 3, 4, 5, 2).reshape(n_layers, n_phys, 2 * WIDTH, page)
    cache_b = cache_b_kv.transpose(0, 1, 3, 4, 5, 2).reshape(n_layers, n_phys, 2 * WIDTH, page)
    cache_i = cache_a_idx.transpose(0, 1, 3, 2)
    seq_tiles = s // tm_p
    map_p = lambda i: (i // seq_tiles, 0, 0)
    map_s = lambda i: (0, i, 0)

    xp, xs = x_prompt.reshape(n_p, d), x_sample.reshape(n_s, d)
    state_p, state_s = [], []
    for l in range(depth):
        mod = _ada(jnp.concatenate([c_prompt, c_sample], axis=0), w_ada[l], b_ada[l]).reshape(b + db, 6, d)
        mod_p = [mod[:b, k][:, None, :] for k in range(6)]
        mod_s = [jnp.repeat(mod[b:, k], t, axis=0)[None] for k in range(6)]

        o = 0
        cols = []
        for width in (WIDTH, 2 * WIDTH, N_IDX_HEADS * IDX_DIM, IDX_DIM + N_IDX_HEADS, WIDTH, 2 * WIDTH, 2 * d):
            cols.append(w_in[l][:, o:o + width])
            o += width
        cols[3] = jnp.pad(cols[3], ((0, 0), (0, LANES - cols[3].shape[1])))
        w_proj = [c.astype(BF16) for c in (cols[0], cols[1], cols[2], cols[3], cols[4], cols[5], cols[6])]
        wr_hi = w_router[l].T.astype(BF16)
        wr_lo = (w_router[l].T - wr_hi.astype(F32)).astype(BF16)
        w_mix = [w_proj_a[l].astype(BF16), w_proj_b[l].astype(BF16), w_out[l].astype(BF16), wr_hi, wr_lo]
        wsh_gu, wsh_dn = w_sh_gu[l].astype(BF16), w_sh_down[l].astype(BF16)

        (qa, qi, qb, sm, ka, kb, ki, ga, gb, kva_t, kvb_t, ki_t, vat, vbt) = _inproj(
            xp, mod_p[1], mod_p[0], map_p, g_pre_mix[l], cos_p, sin_p, seq_tiles, w_proj, tm=tm_p, seq_tiles=seq_tiles)
        oa = _dsa_prompt(qi, qa, sm, ki, ka, vat, b=b, s=s, k_top=min(TOPK_MAX, s // 4))
        ob = _sb_prompt(qb, kb, vbt, b=b, s=s)
        x1_p, h2_p, lg_p = _outproj(oa, ob, ga, gb, xp, mod_p[2], mod_p[4], mod_p[3], map_p,
                                    g_post_mix[l], g_pre_ffn[l], w_mix, tm=tm_p)
        seq_first = lambda v: v.reshape(b, 2, N_HEADS, HEAD_DIM, s).transpose(0, 4, 1, 2, 3)
        state_p.append((seq_first(kva_t), ki_t.transpose(0, 2, 1), seq_first(kvb_t)))

        (qa, qi, qb, sm, _, _, _, ga, gb, kva, kvb) = _inproj(
            xs, mod_s[1], mod_s[0], map_s, g_pre_mix[l], cos_s, sin_s, n_s // tm_s, w_proj, tm=tm_s, seq_tiles=None)
        oa = _dsa_sample(qi.reshape(db, t, WIDTH), qa.reshape(db, t, WIDTH), sm.reshape(db, t, LANES),
                         kva.reshape(db, t, 2 * WIDTH), cache_i, cache_a, page_table, l,
                         k_top=min(TOPK_MAX, (past + t) // 4))
        ob = _sb_sample(qb.reshape(db, t, WIDTH), kvb.reshape(db, t, 2 * WIDTH), cache_b, page_table, l)
        x1_s, h2_s, lg_s = _outproj(oa.reshape(n_s, WIDTH), ob.reshape(n_s, WIDTH), ga, gb, xs, mod_s[2], mod_s[4],
                                    mod_s[3], map_s, g_post_mix[l], g_pre_ffn[l], w_mix, tm=tm_s)
        state_s.append((kva.reshape(db, t, 2, N_HEADS, HEAD_DIM), sm[:, :IDX_DIM].reshape(db, t, IDX_DIM),
                        kvb.reshape(db, t, 2, N_HEADS, HEAD_DIM)))

        n_all = n_p + n_s
        br = MOE_BLOCK_ROWS
        n_blocks = -(-(n_all * EXPERT_TOP_K + n_experts * (br - 1)) // br)
        eidx, gw, rank, counts = _route(jnp.concatenate([lg_p, lg_s], axis=1), b_router[l], tn=tm_p)
        pad_start, block_expert, block_valid = _expert_layout(counts, n_blocks, br)
        dest = _moe_dest(eidx, rank, pad_start, tn=tm_p)
        x_sorted = _sc_scatter_rows(jnp.concatenate([h2_p, h2_s], axis=0), dest, n_blocks * br)
        y_sorted = _experts(x_sorted, block_expert, block_valid, w_exp_gu, w_exp_down, l)
        y_tok = _sc_gather_rows(y_sorted, dest.reshape(-1)).reshape(EXPERT_TOP_K, n_all, d)
        gw_t = gw.T
        xp = _ffn_out(h2_p, y_tok, gw_t, x1_p, mod_p[5], map_p, g_post_ffn[l], wsh_gu, wsh_dn, tm=tm_p)
        xs = _ffn_out(h2_s, y_tok, gw_t, x1_s, mod_s[5], map_s, g_post_ffn[l], wsh_gu, wsh_dn, tm=tm_s, row0=n_p)

    stack = lambda states, k: jnp.stack([st[k] for st in states])
    return (xp.reshape(b, s, d), xs.reshape(db, t, d), stack(state_p, 0), stack(state_p, 1), stack(state_p, 2),
            stack(state_s, 0), stack(state_s, 1), stack(state_s, 2))
```

```python
import functools

import jax
import jax.numpy as jnp
from jax import lax
from jax.experimental import pallas as pl
from jax.experimental.pallas import tpu as pltpu
from jax.experimental.pallas import tpu_sc as plsc

F32 = jnp.float32
BF16 = jnp.bfloat16
I32 = jnp.int32

HEAD_DIM = 64
N_HEADS = 8
WIDTH = N_HEADS * HEAD_DIM
IDX_DIM = 64
N_IDX_HEADS = 8
TOPK_MAX = 256
ROPE_THETA = 10000.0
RMS_EPS = 1e-6
EXPERT_TOP_K = 8
N_EXPERT_GROUPS = 8
TOPK_GROUPS = 4
ROUTED_SCALE = 2.5

LANES = 128
ROW_TILE = 256
ATT_TILE = 256
PAGES_PER_STEP = 8
MOE_BLOCK_ROWS = 256
SC_WORKERS = 32
SC_WINDOW = 40
SC_INDEX_TILE = 128
VMEM_LIMIT = 56 * 1024 * 1024
INT_MIN = -(2 ** 31)
NEG_INF = float("-inf")


def _cparams(*sem):
    return pltpu.CompilerParams(dimension_semantics=sem, vmem_limit_bytes=VMEM_LIMIT)


def _dot(a, b):
    return jnp.dot(a, b, preferred_element_type=F32)


def _dot_nt(a, b):
    return lax.dot_general(a, b, (((1,), (1,)), ((), ())), preferred_element_type=F32)


def _split_bf16(v):
    hi = v.astype(BF16)
    lo = (v - hi.astype(F32)).astype(BF16)
    return hi, lo


def _pack_bf16_pairs(v):
    c = v.shape[1] // 2
    hi = lax.bitcast_convert_type(v[:, :c].astype(BF16).astype(F32), I32)
    lo = lax.bitcast_convert_type(v[:, c:].astype(BF16).astype(F32), I32)
    return hi | lax.shift_right_logical(lo, jnp.int32(16))


def _unpack_bf16_pairs(p):
    hi = lax.bitcast_convert_type(p & jnp.int32(-65536), F32)
    lo = lax.bitcast_convert_type(lax.shift_left(p, jnp.int32(16)), F32)
    return jnp.concatenate([hi, lo], axis=1)


def _rms(v, g):
    return v * lax.rsqrt(jnp.mean(v * v, axis=-1, keepdims=True) + RMS_EPS) * g


def _ada_kernel(c_ref, w_ref, b_ref, o_ref):
    c = c_ref[...]
    a_hi, a_lo = _split_bf16(c * jax.nn.sigmoid(c))
    w_hi, w_lo = _split_bf16(w_ref[...])
    o_ref[...] = _dot(a_hi, w_hi) + _dot(a_hi, w_lo) + _dot(a_lo, w_hi) + b_ref[...]


def _ada(c, w, b):
    m, d = c.shape
    n = w.shape[1]
    tn = 1024 if n % 1024 == 0 else n
    return pl.pallas_call(
        _ada_kernel,
        grid=(n // tn,),
        in_specs=[pl.BlockSpec((m, d), lambda j: (0, 0)),
                  pl.BlockSpec((d, tn), lambda j: (0, j)),
                  pl.BlockSpec((1, tn), lambda j: (0, j))],
        out_specs=pl.BlockSpec((m, tn), lambda j: (0, j)),
        out_shape=jax.ShapeDtypeStruct((m, n), F32),
        compiler_params=_cparams("arbitrary"),
        name="ada_mod",
    )(c, w, b.reshape(1, n))


def _rope_cols(v, cos, sin_signed):
    rows = v.shape[0]
    lane = lax.broadcasted_iota(I32, (rows, LANES), 1)
    first_half = (lane % HEAD_DIM) < (HEAD_DIM // 2)
    outs = []
    for c in range(v.shape[1] // LANES):
        ch = v[:, c * LANES:(c + 1) * LANES]
        partner = jnp.where(first_half, pltpu.roll(ch, LANES - HEAD_DIM // 2, 1), pltpu.roll(ch, HEAD_DIM // 2, 1))
        outs.append(ch * cos + partner * sin_signed)
    return outs[0] if len(outs) == 1 else jnp.concatenate(outs, axis=1)


def _inproj_kernel(x_ref, sc_ref, sh_ref, g_ref, cos_ref, sin_ref,
                   wqa_ref, wkva_ref, wqi_ref, wsm_ref, wqb_ref, wkvb_ref, wg_ref,
                   qa_o, qi_o, qb_o, sm_o, ka_o, kb_o, ki_o, ga_o, gb_o, *state_o):
    d = x_ref.shape[1]
    h = _rms(x_ref[...], g_ref[...]) * (1.0 + sc_ref[0]) + sh_ref[0]
    hb = h.astype(BF16)
    cos = cos_ref[...]
    sin = sin_ref[...]

    qa_o[...] = _rope_cols(_dot(hb, wqa_ref[...]), cos, sin).astype(BF16)
    qi_o[...] = _rope_cols(_dot(hb, wqi_ref[...]), cos, sin).astype(BF16)
    qb_o[...] = _dot(hb, wqb_ref[...]).astype(BF16)

    kva = _dot(hb, wkva_ref[...])
    kva = jnp.concatenate([_rope_cols(kva[:, :WIDTH], cos, sin), kva[:, WIDTH:]], axis=1)
    ka_o[...] = kva[:, :WIDTH].astype(BF16)
    kvb = _dot(hb, wkvb_ref[...])
    kb_o[...] = kvb[:, :WIDTH].astype(BF16)

    sm = _dot(hb, wsm_ref[...])
    lane = lax.broadcasted_iota(I32, sm.shape, 1)
    sm = jnp.where(lane < IDX_DIM, _rope_cols(sm, cos, sin), sm)
    sm_o[...] = sm
    ki_o[...] = sm.astype(BF16)

    g = jax.nn.sigmoid(_dot(hb, wg_ref[...]))
    ga_o[...] = g[:, :d]
    gb_o[...] = g[:, d:]

    if len(state_o) == 2:
        kva_o, kvb_o = state_o
        kva_o[...] = kva
        kvb_o[...] = kvb
    else:
        kvat_o, kvbt_o, kit_o, vat_o, vbt_o = state_o
        kva_t, kvb_t = kva.T, kvb.T
        kvat_o[0] = kva_t
        kvbt_o[0] = kvb_t
        kit_o[0] = sm.T[:IDX_DIM, :]
        vat_o[0, 0] = kva_t[WIDTH:, :].astype(BF16)
        vbt_o[0, 0] = kvb_t[WIDTH:, :].astype(BF16)


def _inproj(x, scale, shift, mod_map, g, cos, sin, tbl_blocks, ws, *, tm, seq_tiles):
    n, d = x.shape
    r = scale.shape[1]
    grid = (n // tm,)
    row = lambda i: (i, 0)
    const = lambda i: (0, 0)
    in_specs = [pl.BlockSpec((tm, d), row),
                pl.BlockSpec((1, r, d), mod_map),
                pl.BlockSpec((1, r, d), mod_map),
                pl.BlockSpec((1, d), const),
                pl.BlockSpec((tm, LANES), lambda i: (i % tbl_blocks, 0)),
                pl.BlockSpec((tm, LANES), lambda i: (i % tbl_blocks, 0))]
    in_specs += [pl.BlockSpec(w.shape, const) for w in ws]
    out_cols = [(WIDTH, BF16), (WIDTH, BF16), (WIDTH, BF16), (LANES, F32),
                (WIDTH, BF16), (WIDTH, BF16), (LANES, BF16), (d, F32), (d, F32)]
    if seq_tiles is None:
        out_cols += [(2 * WIDTH, F32), (2 * WIDTH, F32)]
    out_specs = [pl.BlockSpec((tm, c), row) for c, _ in out_cols]
    out_shape = [jax.ShapeDtypeStruct((n, c), t) for c, t in out_cols]
    if seq_tiles is not None:
        nb = n // (tm * seq_tiles)
        seq_map = lambda i: (i // seq_tiles, 0, i % seq_tiles)
        vt_map = lambda i: (i // seq_tiles, i % seq_tiles, 0, 0)
        out_specs += [pl.BlockSpec((1, 2 * WIDTH, tm), seq_map)] * 2 + [pl.BlockSpec((1, IDX_DIM, tm), seq_map)]
        out_shape += [jax.ShapeDtypeStruct((nb, 2 * WIDTH, tm * seq_tiles), F32)] * 2
        out_shape += [jax.ShapeDtypeStruct((nb, IDX_DIM, tm * seq_tiles), F32)]
        out_specs += [pl.BlockSpec((1, 1, WIDTH, tm), vt_map)] * 2
        out_shape += [jax.ShapeDtypeStruct((nb, seq_tiles, WIDTH, tm), BF16)] * 2
    return pl.pallas_call(
        _inproj_kernel, grid=grid, in_specs=in_specs, out_specs=out_specs, out_shape=out_shape,
        compiler_params=_cparams("arbitrary"), name="in_proj",
    )(x, scale, shift, g.reshape(1, d), cos, sin, *ws)


def _sortable(v):
    b = lax.bitcast_convert_type(v, I32)
    return b ^ ((b >> 31) & jnp.int32(0x7FFFFFFF))


def _select_topk(load_keys, n_chunks, part_sum, final_sum, pos_of_chunk, shape1, part_shape, k_top, idx_bits):
    def count(pred):
        def body(c, acc):
            return acc + part_sum(jnp.where(pred(load_keys(c), c), 1.0, 0.0))
        return final_sum(lax.fori_loop(0, n_chunks, body, jnp.zeros(part_shape, F32), unroll=isinstance(n_chunks, int)))

    def bit_body(bi, t_u):
        cand_u = t_u | jnp.left_shift(jnp.int32(1), 31 - bi)
        cand_s = cand_u ^ jnp.int32(INT_MIN)
        cnt = count(lambda k, c: k >= cand_s)
        return jnp.where(cnt >= k_top, cand_u, t_u)

    t_u = lax.fori_loop(0, 32, bit_body, jnp.zeros(shape1, I32))
    thr = t_u ^ jnp.int32(INT_MIN)
    need = k_top - count(lambda k, c: k > thr)

    def idx_body(bi, cut):
        cand = cut | jnp.left_shift(jnp.int32(1), idx_bits - 1 - bi)
        below = count(lambda k, c: (k == thr) & (pos_of_chunk(c) < cand))
        return jnp.where(below < need, cand, cut)

    n_ge = count(lambda k, c: (k >= thr) & (k != jnp.int32(INT_MIN)))
    split_ties = jnp.max(n_ge) > k_top
    cut = lax.cond(split_ties,
                   lambda: lax.fori_loop(0, idx_bits, idx_body, jnp.zeros(shape1, I32)),
                   lambda: jnp.full(shape1, (1 << idx_bits) - 1, I32))
    return thr, cut


def _dsa_prompt_kernel(qi_ref, qa_ref, sm_ref, ki_ref, ka_ref, vat_ref, o_ref, keys_scr, acc_scr, m_scr, l_scr,
                       *, k_top, idx_bits):
    t = qi_ref.shape[0]
    i = pl.program_id(1)
    n_chunks = i + 1
    q_pos = i * t + lax.broadcasted_iota(I32, (t, t), 1)
    k_iota = lax.broadcasted_iota(I32, (t, t), 0)

    qit = qi_ref[...].astype(F32).T.astype(BF16)
    smt = sm_ref[...].T
    idx_scale = IDX_DIM ** -0.5 * N_IDX_HEADS ** -0.5

    def score_chunk(c, carry):
        off = pl.multiple_of(c * t, t)
        kc = ki_ref[pl.ds(off, t), :][:, :IDX_DIM]
        acc = jnp.zeros((t, t), F32)
        for h in range(N_IDX_HEADS):
            s = _dot(kc, qit[h * IDX_DIM:(h + 1) * IDX_DIM, :])
            acc = acc + jnp.maximum(s, 0.0) * smt[IDX_DIM + h:IDX_DIM + h + 1, :]
        key = _sortable(acc * idx_scale)
        keys_scr[pl.ds(off, t), :] = jnp.where(c * t + k_iota <= q_pos, key, jnp.int32(INT_MIN))
        return carry

    lax.fori_loop(0, n_chunks, score_chunk, 0)

    def load_keys(c):
        return keys_scr[pl.ds(pl.multiple_of(c * t, t), t), :]

    thr, cut = _select_topk(load_keys, n_chunks, lambda x: x.reshape(t // 8, 8, t).sum(axis=0),
                            lambda p: jnp.sum(p, axis=0, keepdims=True), lambda c: c * t + k_iota,
                            (1, t), (8, t), k_top, idx_bits)

    qat = (qa_ref[...].astype(F32) * HEAD_DIM ** -0.5).T.astype(BF16)
    acc_scr[...] = jnp.zeros_like(acc_scr)
    m_scr[...] = jnp.full_like(m_scr, NEG_INF)
    l_scr[...] = jnp.zeros_like(l_scr)

    def att_chunk(c, carry):
        off = pl.multiple_of(c * t, t)
        key = keys_scr[pl.ds(off, t), :]
        sel = ((key > thr) | ((key == thr) & (c * t + k_iota <= cut))) & (key != jnp.int32(INT_MIN))
        k = ka_ref[pl.ds(off, t), :]
        heads = [slice(h * HEAD_DIM, (h + 1) * HEAD_DIM) for h in range(N_HEADS)]
        logits = [jnp.where(sel, _dot(k[:, hs], qat[hs, :]), NEG_INF) for hs in heads]
        m_old = m_scr[...]
        m_new = jnp.maximum(m_old, jnp.concatenate([jnp.max(lt, axis=0, keepdims=True) for lt in logits], axis=0))
        m_safe = jnp.where(m_new == NEG_INF, 0.0, m_new)
        alpha = jnp.exp(m_old - m_safe)
        p = [jnp.exp(lt - m_safe[h:h + 1, :]) for h, lt in enumerate(logits)]
        l_scr[...] = alpha * l_scr[...] + jnp.concatenate([jnp.sum(ph, axis=0, keepdims=True) for ph in p], axis=0)
        m_scr[...] = m_new
        pv = [_dot(vat_ref[c, hs, :], ph.astype(BF16)) for hs, ph in zip(heads, p)]
        for h, hs in enumerate(heads):
            acc_scr[hs, :] = alpha[h:h + 1, :] * acc_scr[hs, :] + pv[h]
        return carry

    lax.fori_loop(0, n_chunks, att_chunk, 0)

    for h in range(N_HEADS):
        hs = slice(h * HEAD_DIM, (h + 1) * HEAD_DIM)
        acc_scr[hs, :] = acc_scr[hs, :] * (1.0 / l_scr[h:h + 1, :])
    o_ref[...] = acc_scr[...].T.astype(BF16)


def _dsa_prompt(qi, qa, sm, ki, ka, vat, *, b, s, k_top):
    t = ATT_TILE
    nq = s // t
    qmap = lambda bi, i: (bi * nq + i, 0)
    kmap = lambda bi, i: (bi, 0)
    return pl.pallas_call(
        functools.partial(_dsa_prompt_kernel, k_top=k_top, idx_bits=max(1, (s - 1).bit_length())),
        grid=(b, nq),
        in_specs=[pl.BlockSpec((t, WIDTH), qmap), pl.BlockSpec((t, WIDTH), qmap), pl.BlockSpec((t, LANES), qmap),
                  pl.BlockSpec((s, LANES), kmap), pl.BlockSpec((s, WIDTH), kmap),
                  pl.BlockSpec((None, nq, WIDTH, t), lambda bi, i: (bi, 0, 0, 0))],
        out_specs=pl.BlockSpec((t, WIDTH), qmap),
        out_shape=jax.ShapeDtypeStruct((b * s, WIDTH), BF16),
        scratch_shapes=[pltpu.VMEM((s, t), I32), pltpu.VMEM((WIDTH, t), F32),
                        pltpu.VMEM((N_HEADS, t), F32), pltpu.VMEM((N_HEADS, t), F32)],
        compiler_params=_cparams("arbitrary", "arbitrary"),
        name="dsa_prompt",
    )(qi, qa, sm, ki, ka, vat)


def _log_sigmoid_pair(z):
    tail = jnp.log(1.0 + jnp.exp(-jnp.abs(z)))
    return jnp.minimum(z, 0.0) - tail, jnp.minimum(-z, 0.0) - tail


def _sb_prompt_kernel(qb_ref, kb_ref, vbt_ref, o_ref, acc_scr, carry_scr):
    t = qb_ref.shape[0]
    i = pl.program_id(1)
    q_pos = i * t + lax.broadcasted_iota(I32, (t, t), 1)
    k_iota = lax.broadcasted_iota(I32, (t, t), 0)
    later = (lax.broadcasted_iota(I32, (t, t), 1) > k_iota).astype(BF16)
    qt = (qb_ref[...].astype(F32) * HEAD_DIM ** -0.5).T.astype(BF16)
    acc_scr[...] = jnp.zeros_like(acc_scr)
    carry_scr[...] = jnp.zeros_like(carry_scr)

    heads = [slice(h * HEAD_DIM, (h + 1) * HEAD_DIM) for h in range(N_HEADS)]

    def chunk(c, mask):
        off = pl.multiple_of(c * t, t)
        k = kb_ref[pl.ds(off, t), :]
        z = [_dot(k[:, hs], qt[hs, :]) for hs in heads]
        pairs = [_log_sigmoid_pair(zh) for zh in z]
        log_keep = [lk if mask is None else jnp.where(mask, lk, 0.0) for _, lk in pairs]
        split = [_split_bf16(lk) for lk in log_keep]
        suffix = [_dot(later, hi) + _dot(later, lo) + carry_scr[h:h + 1, :] for h, (hi, lo) in enumerate(split)]
        w = [jnp.exp(lb + sf) for (lb, _), sf in zip(pairs, suffix)]
        if mask is not None:
            w = [jnp.where(mask, wh, 0.0) for wh in w]
        pv = [_dot(vbt_ref[c, hs, :], wh.astype(BF16)) for hs, wh in zip(heads, w)]
        for h, hs in enumerate(heads):
            acc_scr[hs, :] = acc_scr[hs, :] + pv[h]
            carry_scr[h:h + 1, :] = carry_scr[h:h + 1, :] + jnp.sum(log_keep[h], axis=0, keepdims=True)

    chunk(i, i * t + k_iota < q_pos)

    def earlier_chunk(j, carry):
        chunk(i - 1 - j, None)
        return carry

    lax.fori_loop(0, i, earlier_chunk, 0)
    o_ref[...] = acc_scr[...].T.astype(BF16)


def _sb_prompt(qb, kb, vbt, *, b, s):
    t = ATT_TILE
    nq = s // t
    qmap = lambda bi, i: (bi * nq + i, 0)
    return pl.pallas_call(
        _sb_prompt_kernel,
        grid=(b, nq),
        in_specs=[pl.BlockSpec((t, WIDTH), qmap), pl.BlockSpec((s, WIDTH), lambda bi, i: (bi, 0)),
                  pl.BlockSpec((None, nq, WIDTH, t), lambda bi, i: (bi, 0, 0, 0))],
        out_specs=pl.BlockSpec((t, WIDTH), qmap),
        out_shape=jax.ShapeDtypeStruct((b * s, WIDTH), BF16),
        scratch_shapes=[pltpu.VMEM((WIDTH, t), F32), pltpu.VMEM((N_HEADS, t), F32)],
        compiler_params=_cparams("arbitrary", "arbitrary"),
        name="sb_prompt",
    )(qb, kb, vbt)


def _rows_by_head(q):
    return jnp.concatenate([q[:, h * HEAD_DIM:(h + 1) * HEAD_DIM] for h in range(N_HEADS)], axis=0)


def _block_diag_rows(q):
    t = q.shape[0]
    rows = jnp.concatenate([q] * N_HEADS, axis=0)
    r = lax.broadcasted_iota(I32, rows.shape, 0) // t
    c = lax.broadcasted_iota(I32, rows.shape, 1) // HEAD_DIM
    return jnp.where(r == c, rows, 0.0)


def _diag_heads(acc, t):
    return jnp.concatenate([acc[h * t:(h + 1) * t, h * HEAD_DIM:(h + 1) * HEAD_DIM] for h in range(N_HEADS)], axis=1)


def _pad_rows(v, rows):
    return jnp.concatenate([v, jnp.zeros((rows - v.shape[0], v.shape[1]), v.dtype)], axis=0)


def _pages_kv_t(pages):
    k_t = jnp.concatenate([pg[:WIDTH, :] for pg in pages], axis=1).astype(BF16)
    v_t = jnp.concatenate([pg[WIDTH:, :] for pg in pages], axis=1).astype(BF16)
    return k_t, v_t


def _new_kv_t(kv_new, page):
    kv_t = _pad_rows(kv_new, page).T.astype(BF16)
    return kv_t[:WIDTH, :], kv_t[WIDTH:, :]


def _dsa_sample_select_kernel(pt_ref, *refs, k_top, idx_bits, n_steps):
    pages = refs[:PAGES_PER_STEP]
    qi_ref, sm_ref, bias_o, keys_scr = refs[PAGES_PER_STEP:]
    j = pl.program_id(1)
    t = qi_ref.shape[0]
    page = pages[0].shape[1]
    cw = PAGES_PER_STEP * page
    idx_scale = IDX_DIM ** -0.5 * N_IDX_HEADS ** -0.5
    q_rows = _rows_by_head(qi_ref[...].astype(F32)).astype(BF16)
    sm = sm_ref[...]
    w_cols = [jnp.broadcast_to(sm[:, IDX_DIM + h:IDX_DIM + h + 1], (t, page)) for h in range(N_IDX_HEADS)]

    def scores(k_t):
        s = _dot(q_rows, k_t)
        tiles = []
        for c in range(k_t.shape[1] // page):
            acc = jnp.zeros((t, page), F32)
            for h in range(N_IDX_HEADS):
                acc = acc + jnp.maximum(s[h * t:(h + 1) * t, c * page:(c + 1) * page], 0.0) * w_cols[h]
            tiles.append(acc * idx_scale)
        return tiles[0] if len(tiles) == 1 else jnp.concatenate(tiles, axis=1)

    keys_scr[j] = _sortable(scores(jnp.concatenate([pg[...] for pg in pages], axis=1).astype(BF16)))

    @pl.when(j == n_steps - 1)
    def _():
        k_new = _pad_rows(sm[:, :IDX_DIM], page).T.astype(BF16)
        key = _sortable(scores(k_new))
        u = lax.broadcasted_iota(I32, (t, page), 1)
        q = lax.broadcasted_iota(I32, (t, page), 0)
        keys_scr[n_steps, :, :page] = jnp.where(u <= q, key, jnp.int32(INT_MIN))
        keys_scr[n_steps, :, page:] = jnp.full((t, cw - page), INT_MIN, I32)

        lane = lax.broadcasted_iota(I32, (t, cw), 1)
        part = lambda x: functools.reduce(lambda a, b2: a + b2, [x[:, k * page:(k + 1) * page] for k in range(PAGES_PER_STEP)])
        final = lambda p: jnp.sum(p, axis=1, keepdims=True)
        thr, cut = _select_topk(lambda c: keys_scr[c], n_steps + 1, part, final, lambda c: c * cw + lane,
                                (t, 1), (t, page), k_top, idx_bits)

        for c in range(n_steps + 1):
            key = keys_scr[c]
            sel = ((key > thr) | ((key == thr) & (c * cw + lane <= cut))) & (key != jnp.int32(INT_MIN))
            bias_o[c] = jnp.where(sel, 0.0, NEG_INF)


def _dsa_sample_attend_kernel(pt_ref, *refs, n_steps):
    pages = refs[:PAGES_PER_STEP]
    qa_ref, kv_new_ref, bias_ref, bias_new_ref, o_ref, acc_scr, m_scr, l_scr = refs[PAGES_PER_STEP:]
    j = pl.program_id(1)
    t = qa_ref.shape[0]
    page = pages[0].shape[1]
    q_bd = _block_diag_rows(qa_ref[...].astype(F32) * HEAD_DIM ** -0.5).astype(BF16)

    @pl.when(j == 0)
    def _():
        acc_scr[...] = jnp.zeros_like(acc_scr)
        m_scr[...] = jnp.full_like(m_scr, NEG_INF)
        l_scr[...] = jnp.zeros_like(l_scr)

    def update(k_t, v_t, bias):
        logits = _dot(q_bd, k_t) + jnp.concatenate([bias] * N_HEADS, axis=0)
        m_old = m_scr[...]
        m_new = jnp.maximum(m_old, jnp.max(logits, axis=1, keepdims=True))
        m_safe = jnp.where(m_new == NEG_INF, 0.0, m_new)
        alpha = jnp.exp(m_old - m_safe)
        p = jnp.exp(logits - m_safe)
        l_scr[...] = alpha * l_scr[...] + jnp.sum(p, axis=1, keepdims=True)
        m_scr[...] = m_new
        acc_scr[...] = alpha * acc_scr[...] + _dot_nt(p.astype(BF16), v_t)

    update(*_pages_kv_t(pages), bias_ref[...])

    @pl.when(j == n_steps - 1)
    def _():
        update(*_new_kv_t(kv_new_ref[...], page), bias_new_ref[:, :page])
        o_ref[...] = _diag_heads(acc_scr[...] * (1.0 / l_scr[...]), t).astype(BF16)


def _page_specs(cache, layer, n_pages, reverse=False):
    page, width = cache.shape[2:]

    def spec(r):
        if reverse:
            return pl.BlockSpec((None, None, page, width),
                                lambda b, j, pt: (layer, pt[b, n_pages - 1 - (j * PAGES_PER_STEP + r)], 0, 0))
        return pl.BlockSpec((None, None, page, width), lambda b, j, pt: (layer, pt[b, j * PAGES_PER_STEP + r], 0, 0))
    return [spec(r) for r in range(PAGES_PER_STEP)]


def _dsa_sample(qi, qa, sm, kva_new, cache_idx, cache_kv, page_table, layer, *, k_top):
    db, t, _ = qi.shape
    n_pages = page_table.shape[1]
    n_steps = n_pages // PAGES_PER_STEP
    page = cache_idx.shape[3]
    cw = PAGES_PER_STEP * page
    per_b = lambda width: pl.BlockSpec((None, t, width), lambda b, j, pt: (b, 0, 0))
    n_keys = n_pages * page + t
    bias = pl.pallas_call(
        functools.partial(_dsa_sample_select_kernel, k_top=k_top, idx_bits=max(1, (n_keys - 1).bit_length()), n_steps=n_steps),
        grid_spec=pltpu.PrefetchScalarGridSpec(
            num_scalar_prefetch=1, grid=(db, n_steps),
            in_specs=_page_specs(cache_idx, layer, n_pages) + [per_b(WIDTH), per_b(LANES)],
            out_specs=pl.BlockSpec((None, n_steps + 1, t, cw), lambda b, j, pt: (b, 0, 0, 0)),
            scratch_shapes=[pltpu.VMEM((n_steps + 1, t, cw), I32)]),
        out_shape=jax.ShapeDtypeStruct((db, n_steps + 1, t, cw), F32),
        compiler_params=_cparams("arbitrary", "arbitrary"),
        name="dsa_sample_select",
    )(page_table, *([cache_idx] * PAGES_PER_STEP), qi, sm)
    return pl.pallas_call(
        functools.partial(_dsa_sample_attend_kernel, n_steps=n_steps),
        grid_spec=pltpu.PrefetchScalarGridSpec(
            num_scalar_prefetch=1, grid=(db, n_steps),
            in_specs=_page_specs(cache_kv, layer, n_pages) + [
                per_b(WIDTH), per_b(2 * WIDTH),
                pl.BlockSpec((None, None, t, cw), lambda b, j, pt: (b, j, 0, 0)),
                pl.BlockSpec((None, None, t, cw), lambda b, j, pt: (b, n_steps, 0, 0))],
            out_specs=per_b(WIDTH),
            scratch_shapes=[pltpu.VMEM((N_HEADS * t, WIDTH), F32), pltpu.VMEM((N_HEADS * t, 1), F32),
                            pltpu.VMEM((N_HEADS * t, 1), F32)]),
        out_shape=jax.ShapeDtypeStruct((db, t, WIDTH), BF16),
        compiler_params=_cparams("arbitrary", "arbitrary"),
        name="dsa_sample_attend",
    )(page_table, *([cache_kv] * PAGES_PER_STEP), qa, kva_new, bias, bias)


def _suffix_scan_lanes(x):
    lane = lax.broadcasted_iota(I32, x.shape, 1)
    sh = 1
    while sh < LANES:
        x = x + jnp.where(lane + sh < LANES, pltpu.roll(x, LANES - sh, 1), 0.0)
        sh *= 2
    return x


def _sb_sample_kernel(pt_ref, *refs, n_steps):
    pages = refs[:PAGES_PER_STEP]
    qb_ref, kv_new_ref, o_ref, acc_scr, carry_scr = refs[PAGES_PER_STEP:]
    j = pl.program_id(1)
    t = qb_ref.shape[0]
    page = pages[0].shape[1]
    q_bd = _block_diag_rows(qb_ref[...].astype(F32) * HEAD_DIM ** -0.5).astype(BF16)

    def weights(z, mask):
        outs = []
        carry = carry_scr[...]
        for k in range(z.shape[1] // page):
            zt = z[:, k * page:(k + 1) * page]
            log_beta, log_keep = _log_sigmoid_pair(zt)
            if mask is not None:
                log_keep = jnp.where(mask, log_keep, 0.0)
            incl = _suffix_scan_lanes(log_keep)
            w = jnp.exp(log_beta + (incl - log_keep) + carry)
            if mask is not None:
                w = jnp.where(mask, w, 0.0)
            outs.append(w)
            carry = carry + incl[:, :1]
        carry_scr[...] = carry
        return outs[0] if len(outs) == 1 else jnp.concatenate(outs, axis=1)

    @pl.when(j == 0)
    def _():
        carry_scr[...] = jnp.zeros_like(carry_scr)
        k_t, v_t = _new_kv_t(kv_new_ref[...], page)
        u = lax.broadcasted_iota(I32, (N_HEADS * t, page), 1)
        q = lax.broadcasted_iota(I32, (N_HEADS * t, page), 0) % t
        w = weights(_dot(q_bd, k_t), u < q)
        acc_scr[...] = _dot_nt(w.astype(BF16), v_t)

    k_t, v_t = _pages_kv_t(pages)
    w = weights(_dot(q_bd, k_t), None)
    acc_scr[...] = acc_scr[...] + _dot_nt(w.astype(BF16), v_t)

    @pl.when(j == n_steps - 1)
    def _():
        o_ref[...] = _diag_heads(acc_scr[...], t).astype(BF16)


def _sb_sample(qb, kvb_new, cache_kv, page_table, layer):
    db, t, _ = qb.shape
    n_pages = page_table.shape[1]
    n_steps = n_pages // PAGES_PER_STEP
    per_b = lambda width: pl.BlockSpec((None, t, width), lambda b, j, pt: (b, 0, 0))
    return pl.pallas_call(
        functools.partial(_sb_sample_kernel, n_steps=n_steps),
        grid_spec=pltpu.PrefetchScalarGridSpec(
            num_scalar_prefetch=1, grid=(db, n_steps),
            in_specs=_page_specs(cache_kv, layer, n_pages, reverse=True) + [per_b(WIDTH), per_b(2 * WIDTH)],
            out_specs=per_b(WIDTH),
            scratch_shapes=[pltpu.VMEM((N_HEADS * t, WIDTH), F32), pltpu.VMEM((N_HEADS * t, 1), F32)]),
        out_shape=jax.ShapeDtypeStruct((db, t, WIDTH), BF16),
        compiler_params=_cparams("arbitrary", "arbitrary"),
        name="sb_sample",
    )(page_table, *([cache_kv] * PAGES_PER_STEP), qb, kvb_new)


def _outproj_kernel(oa_ref, ob_ref, ga_ref, gb_ref, x_ref, gate_ref, sc_ref, sh_ref, gpost_ref, gpre_ref,
                    wpa_ref, wpb_ref, wout_ref, wrh_ref, wrl_ref, x1_o, h2_o, lg_o):
    mix = ga_ref[...] * _dot(oa_ref[...], wpa_ref[...]) + gb_ref[...] * _dot(ob_ref[...], wpb_ref[...])
    mo = _dot(mix.astype(BF16), wout_ref[...])
    x1 = x_ref[...] + gate_ref[0] * _rms(mo, gpost_ref[...])
    x1_o[...] = x1
    h2 = _rms(x1, gpre_ref[...]) * (1.0 + sc_ref[0]) + sh_ref[0]
    h_hi, h_lo = _split_bf16(h2)
    h2_o[...] = _pack_bf16_pairs(h2)
    lg_o[...] = _dot_nt(wrh_ref[...], h_hi) + _dot_nt(wrh_ref[...], h_lo) + _dot_nt(wrl_ref[...], h_hi)


def _outproj(oa, ob, ga, gb, x, gate, scale, shift, mod_map, gpost, gpre, ws, *, tm):
    n, d = x.shape
    r = gate.shape[1]
    e = ws[3].shape[0]
    row = lambda i: (i, 0)
    const = lambda i: (0, 0)
    in_specs = [pl.BlockSpec((tm, WIDTH), row), pl.BlockSpec((tm, WIDTH), row),
                pl.BlockSpec((tm, d), row), pl.BlockSpec((tm, d), row), pl.BlockSpec((tm, d), row),
                pl.BlockSpec((1, r, d), mod_map), pl.BlockSpec((1, r, d), mod_map), pl.BlockSpec((1, r, d), mod_map),
                pl.BlockSpec((1, d), const), pl.BlockSpec((1, d), const)]
    in_specs += [pl.BlockSpec(w.shape, const) for w in ws]
    return pl.pallas_call(
        _outproj_kernel, grid=(n // tm,), in_specs=in_specs,
        out_specs=[pl.BlockSpec((tm, d), row), pl.BlockSpec((tm, d // 2), row), pl.BlockSpec((e, tm), lambda i: (0, i))],
        out_shape=[jax.ShapeDtypeStruct((n, d), F32), jax.ShapeDtypeStruct((n, d // 2), I32),
                   jax.ShapeDtypeStruct((e, n), F32)],
        compiler_params=_cparams("arbitrary"), name="out_proj",
    )(oa, ob, ga, gb, x, gate, scale, shift, gpost.reshape(1, d), gpre.reshape(1, d), *ws)


def _route_kernel(lg_ref, b_ref, eidx_o, gw_o, rank_o, count_o, count_scr):
    @pl.when(pl.program_id(0) == 0)
    def _():
        count_scr[...] = jnp.zeros_like(count_scr)

    scores = jax.nn.sigmoid(lg_ref[...])
    biased = scores + b_ref[...]
    e, tn = scores.shape
    gsz = e // N_EXPERT_GROUPS
    group_score = []
    for g in range(N_EXPERT_GROUPS):
        xg = biased[g * gsz:(g + 1) * gsz, :]
        m1 = jnp.max(xg, axis=0, keepdims=True)
        n1 = jnp.sum(jnp.where(xg == m1, 1.0, 0.0), axis=0, keepdims=True)
        m2 = jnp.max(jnp.where(xg < m1, xg, NEG_INF), axis=0, keepdims=True)
        group_score.append(m1 + jnp.where(n1 >= 2.0, m1, m2))
    cur = []
    for g in range(N_EXPERT_GROUPS):
        rank = jnp.zeros((1, tn), F32)
        for g2 in range(N_EXPERT_GROUPS):
            if g2 == g:
                continue
            ahead = (group_score[g2] > group_score[g]) | ((group_score[g2] == group_score[g]) & (g2 < g))
            rank = rank + jnp.where(ahead, 1.0, 0.0)
        cur.append(jnp.where(rank < TOPK_GROUPS, biased[g * gsz:(g + 1) * gsz, :], NEG_INF))
    cur = jnp.concatenate(cur, axis=0)
    row = lax.broadcasted_iota(I32, (e, tn), 0).astype(F32)
    picked, hits = [], []
    for k in range(EXPERT_TOP_K):
        m = jnp.max(cur, axis=0, keepdims=True)
        idx = jnp.min(jnp.where(cur == m, row, float(e)), axis=0, keepdims=True)
        hit = row == idx
        hits.append(hit)
        eidx_o[k:k + 1, :] = idx.astype(I32)
        picked.append(jnp.sum(jnp.where(hit, scores, 0.0), axis=0, keepdims=True))
        cur = jnp.where(hit, NEG_INF, cur)
    total = functools.reduce(lambda a, b2: a + b2, picked)
    for k in range(EXPERT_TOP_K):
        gw_o[k:k + 1, :] = picked[k] / total * ROUTED_SCALE

    onehot = functools.reduce(lambda a, b2: a | b2, hits)
    earlier = lax.broadcasted_iota(I32, (tn, tn), 0) < lax.broadcasted_iota(I32, (tn, tn), 1)
    before = _dot(jnp.where(onehot, 1.0, 0.0).astype(BF16), earlier.astype(BF16)) + count_scr[...]
    for k in range(EXPERT_TOP_K):
        rank_o[k:k + 1, :] = jnp.sum(jnp.where(hits[k], before, 0.0), axis=0, keepdims=True).astype(I32)
    count_scr[...] = count_scr[...] + jnp.sum(jnp.where(onehot, 1.0, 0.0), axis=1, keepdims=True)
    count_o[...] = count_scr[...]


def _route(logits_t, b_router, *, tn):
    e, n = logits_t.shape
    tok = pl.BlockSpec((EXPERT_TOP_K, tn), lambda i: (0, i))
    return pl.pallas_call(
        _route_kernel, grid=(n // tn,),
        in_specs=[pl.BlockSpec((e, tn), lambda i: (0, i)), pl.BlockSpec((e, 1), lambda i: (0, 0))],
        out_specs=[tok, tok, tok, pl.BlockSpec((e, 1), lambda i: (0, 0))],
        out_shape=[jax.ShapeDtypeStruct((EXPERT_TOP_K, n), I32), jax.ShapeDtypeStruct((EXPERT_TOP_K, n), F32),
                   jax.ShapeDtypeStruct((EXPERT_TOP_K, n), I32), jax.ShapeDtypeStruct((e, 1), F32)],
        scratch_shapes=[pltpu.VMEM((e, 1), F32)],
        compiler_params=_cparams("arbitrary"), name="moe_route",
    )(logits_t, b_router.reshape(e, 1))


def _moe_dest_kernel(eidx_ref, rank_ref, start_ref, dest_o):
    e = start_ref.shape[0]
    tn = eidx_ref.shape[1]
    row = lax.broadcasted_iota(I32, (e, tn), 0)
    start = start_ref[...]
    for k in range(EXPERT_TOP_K):
        base = jnp.sum(jnp.where(row == eidx_ref[k:k + 1, :], start, 0.0), axis=0, keepdims=True)
        dest_o[k:k + 1, :] = base.astype(I32) + rank_ref[k:k + 1, :]


def _moe_dest(eidx, rank, pad_start, *, tn):
    k, n = eidx.shape
    e = pad_start.shape[0]
    tok = pl.BlockSpec((k, tn), lambda i: (0, i))
    return pl.pallas_call(
        _moe_dest_kernel, grid=(n // tn,),
        in_specs=[tok, tok, pl.BlockSpec((e, 1), lambda i: (0, 0))],
        out_specs=tok, out_shape=jax.ShapeDtypeStruct((k, n), I32),
        compiler_params=_cparams("arbitrary"), name="moe_dest",
    )(eidx, rank, pad_start)


def _experts_kernel(be_ref, bv_ref, x_ref, wgu_ref, wdn_ref, y_o, wgu_scr, wdn_scr):
    i = pl.program_id(0)
    br = x_ref.shape[0]
    de = wdn_ref.shape[0]
    valid = bv_ref[i]

    @pl.when((i == 0) | (be_ref[i] != be_ref[jnp.maximum(i - 1, 0)]))
    def _():
        wgu_scr[...] = wgu_ref[...].astype(BF16)
        wdn_scr[...] = wdn_ref[...].astype(BF16)

    @pl.when(valid > 0)
    def _():
        live = lax.broadcasted_iota(I32, (br, 1), 0) < valid
        x = _unpack_bf16_pairs(jnp.where(live, x_ref[...], 0)).astype(BF16)
        gu = _dot(x, wgu_scr[...])
        gate = gu[:, :de]
        act = gate * jax.nn.sigmoid(gate) * gu[:, de:]
        y_o[...] = _pack_bf16_pairs(_dot(act.astype(BF16), wdn_scr[...]))

    @pl.when(valid == 0)
    def _():
        y_o[...] = jnp.zeros_like(y_o)


def _experts(x_sorted, block_expert, block_valid, w_gu, w_dn, layer):
    rows, dp = x_sorted.shape
    d = 2 * dp
    de = w_dn.shape[2]
    br = MOE_BLOCK_ROWS
    return pl.pallas_call(
        _experts_kernel,
        grid_spec=pltpu.PrefetchScalarGridSpec(
            num_scalar_prefetch=2, grid=(rows // br,),
            in_specs=[pl.BlockSpec((br, dp), lambda i, be, bv: (i, 0)),
                      pl.BlockSpec((None, None, d, 2 * de), lambda i, be, bv: (layer, be[i], 0, 0)),
                      pl.BlockSpec((None, None, de, d), lambda i, be, bv: (layer, be[i], 0, 0))],
            out_specs=pl.BlockSpec((br, dp), lambda i, be, bv: (i, 0)),
            scratch_shapes=[pltpu.VMEM((d, 2 * de), BF16), pltpu.VMEM((de, d), BF16)]),
        out_shape=jax.ShapeDtypeStruct((rows, dp), I32),
        compiler_params=_cparams("arbitrary"), name="moe_experts",
    )(block_expert, block_valid, x_sorted, w_gu, w_dn)


def _ffn_out_kernel(h2_ref, yg_ref, gw_ref, x1_ref, gate_ref, gpost_ref, wgu_ref, wdn_ref, y_o):
    ds = wdn_ref.shape[0]
    gu = _dot(_unpack_bf16_pairs(h2_ref[...]).astype(BF16), wgu_ref[...])
    gate = gu[:, :ds]
    act = gate * jax.nn.sigmoid(gate) * gu[:, ds:]
    f = _dot(act.astype(BF16), wdn_ref[...])
    gw = gw_ref[...]
    for k in range(EXPERT_TOP_K):
        f = f + _unpack_bf16_pairs(yg_ref[k]) * gw[:, k:k + 1]
    y_o[...] = x1_ref[...] + gate_ref[0] * _rms(f, gpost_ref[...])


def _ffn_out(h2, y_gathered, gw_t, x1, gate, mod_map, gpost, wgu, wdn, *, tm, row0=0):
    n, d = x1.shape
    r = gate.shape[1]
    k = y_gathered.shape[0]
    blk0 = row0 // tm
    row = lambda i: (i, 0)
    const = lambda i: (0, 0)
    return pl.pallas_call(
        _ffn_out_kernel, grid=(n // tm,),
        in_specs=[pl.BlockSpec((tm, d // 2), row), pl.BlockSpec((k, tm, d // 2), lambda i: (0, blk0 + i, 0)),
                  pl.BlockSpec((tm, k), lambda i: (blk0 + i, 0)), pl.BlockSpec((tm, d), row),
                  pl.BlockSpec((1, r, d), mod_map), pl.BlockSpec((1, d), const),
                  pl.BlockSpec(wgu.shape, const), pl.BlockSpec(wdn.shape, const)],
        out_specs=pl.BlockSpec((tm, d), row),
        out_shape=jax.ShapeDtypeStruct((n, d), F32),
        compiler_params=_cparams("arbitrary"), name="ffn_out",
    )(h2, y_gathered, gw_t, x1, gate, gpost.reshape(1, d), wgu, wdn)


def _sc_index_rows(idx):
    steps = idx.shape[0] // SC_WINDOW
    assert idx.shape[0] % SC_WINDOW == 0 and steps % SC_WORKERS == 0
    return jnp.pad(idx.reshape(steps, SC_WINDOW), ((0, 0), (0, SC_INDEX_TILE - SC_WINDOW)))


def _sc_scatter_rows(src, dest, rows_out):
    n, d = src.shape
    k = dest.shape[0]
    steps = n // SC_WINDOW
    index_rows = _sc_index_rows(dest.reshape(-1))
    mesh = plsc.VectorSubcoreMesh(core_axis_name="core", subcore_axis_name="subcore")

    @pl.kernel(out_type=jax.ShapeDtypeStruct((rows_out, d), src.dtype), mesh=mesh, scratch_types=[])
    def scatter(src_hbm, idx_hbm, out_hbm):
        def step(src_vmem, idx_vmem):
            pltpu.sync_copy(src_vmem, out_hbm.at[idx_vmem.at[0, pl.ds(0, SC_WINDOW)]])

        pltpu.emit_pipeline(
            step, grid=(steps, k),
            in_specs=[pl.BlockSpec((SC_WINDOW, d), lambda i, kk: (i, 0)),
                      pl.BlockSpec((1, SC_INDEX_TILE), lambda i, kk: (kk * steps + i, 0))],
            out_specs=[],
            core_axis_name=("core", "subcore"),
            dimension_semantics=(pltpu.PARALLEL, pltpu.ARBITRARY),
        )(src_hbm, idx_hbm)

    return scatter(src, index_rows)


def _sc_gather_rows(src, idx):
    m = idx.shape[0]
    d = src.shape[1]
    index_rows = _sc_index_rows(idx)
    mesh = plsc.VectorSubcoreMesh(core_axis_name="core", subcore_axis_name="subcore")

    @pl.kernel(out_type=jax.ShapeDtypeStruct((m, d), src.dtype), mesh=mesh, scratch_types=[])
    def gather(src_hbm, idx_hbm, out_hbm):
        def step(idx_vmem, out_vmem):
            pltpu.sync_copy(src_hbm.at[idx_vmem.at[0, pl.ds(0, SC_WINDOW)]], out_vmem)

        pltpu.emit_pipeline(
            step, grid=(m // SC_WINDOW,),
            in_specs=[pl.BlockSpec((1, SC_INDEX_TILE), lambda i: (i, 0))],
            out_specs=[pl.BlockSpec((SC_WINDOW, d), lambda i: (i, 0))],
            core_axis_name=("core", "subcore"),
            dimension_semantics=(pltpu.PARALLEL,),
        )(idx_hbm, out_hbm)

    return gather(src, index_rows)


def _rope_tables(pos):
    half = HEAD_DIM // 2
    inv_freq = ROPE_THETA ** (-jnp.arange(half, dtype=F32) / half)
    ang = pos.astype(F32)[:, None] * inv_freq[None, :]
    cos, sin = jnp.cos(ang), jnp.sin(ang)
    reps = LANES // HEAD_DIM
    return (jnp.tile(jnp.concatenate([cos, cos], axis=1), (1, reps)),
            jnp.tile(jnp.concatenate([-sin, sin], axis=1), (1, reps)))


def _expert_layout(counts, n_blocks, br):
    n_experts = counts.shape[0]
    count = counts[:, 0].astype(I32)
    padded = (count + br - 1) // br * br
    pad_end = jnp.cumsum(padded)
    pad_start = pad_end - padded
    block_start = jnp.arange(n_blocks, dtype=I32) * br
    block_expert = jnp.minimum(jnp.sum((pad_end[None, :] <= block_start[:, None]).astype(I32), axis=1), n_experts - 1)
    in_expert = (jnp.arange(n_experts, dtype=I32)[None, :] == block_expert[:, None]) & (block_start[:, None] < pad_end[None, :])
    live_end = jnp.sum(jnp.where(in_expert, (pad_start + count)[None, :], 0), axis=1)
    block_valid = jnp.clip(live_end - block_start, 0, br).astype(I32)
    return pad_start.astype(F32)[:, None], block_expert, block_valid


def kernel(x_prompt, x_sample, c_prompt, c_sample, cache_a_kv, cache_a_idx, cache_b_kv, page_table, w_ada, b_ada,
           g_pre_mix, g_post_mix, g_pre_ffn, g_post_ffn, w_in, w_proj_a, w_proj_b, w_out, w_router, b_router,
           w_exp_gu, w_exp_down, w_sh_gu, w_sh_down):
    b, s, d = x_prompt.shape
    db, t, _ = x_sample.shape
    n_p, n_s = b * s, db * t
    depth = w_ada.shape[0]
    n_layers, n_phys, page = cache_a_kv.shape[:3]
    past = page_table.shape[1] * page
    n_experts = w_router.shape[2]
    tm_p = ROW_TILE
    tm_s = min(ROW_TILE, n_s)
    assert s % ATT_TILE == 0 and ATT_TILE == tm_p and n_s % tm_s == 0 and (n_p + n_s) % tm_p == 0
    assert page_table.shape[1] % PAGES_PER_STEP == 0 and t <= page

    cos_p, sin_p = _rope_tables(jnp.arange(s, dtype=jnp.int32))
    cos_s, sin_s = _rope_tables(past + jnp.arange(t, dtype=jnp.int32))
    cos_s, sin_s = jnp.tile(cos_s, (db, 1)), jnp.tile(sin_s, (db, 1))
    cache_a = cache_a_kv.transpose(0, 1, 3, 4, 5, 2).reshape(n_layers, n_phys, 2 * WIDTH, page)
    cache_b = cache_b_kv.transpose(0, 1, 3, 4, 5, 2).reshape(n_layers, n_phys, 2 * WIDTH, page)
    cache_i = cache_a_idx.transpose(0, 1, 3, 2)
    seq_tiles = s // tm_p
    map_p = lambda i: (i // seq_tiles, 0, 0)
    map_s = lambda i: (0, i, 0)

    xp, xs = x_prompt.reshape(n_p, d), x_sample.reshape(n_s, d)
    state_p, state_s = [], []
    for l in range(depth):
        mod = _ada(jnp.concatenate([c_prompt, c_sample], axis=0), w_ada[l], b_ada[l]).reshape(b + db, 6, d)
        mod_p = [mod[:b, k][:, None, :] for k in range(6)]
        mod_s = [jnp.repeat(mod[b:, k], t, axis=0)[None] for k in range(6)]

        o = 0
        cols = []
        for width in (WIDTH, 2 * WIDTH, N_IDX_HEADS * IDX_DIM, IDX_DIM + N_IDX_HEADS, WIDTH, 2 * WIDTH, 2 * d):
            cols.append(w_in[l][:, o:o + width])
            o += width
        cols[3] = jnp.pad(cols[3], ((0, 0), (0, LANES - cols[3].shape[1])))
        w_proj = [c.astype(BF16) for c in (cols[0], cols[1], cols[2], cols[3], cols[4], cols[5], cols[6])]
        wr_hi = w_router[l].T.astype(BF16)
        wr_lo = (w_router[l].T - wr_hi.astype(F32)).astype(BF16)
        w_mix = [w_proj_a[l].astype(BF16), w_proj_b[l].astype(BF16), w_out[l].astype(BF16), wr_hi, wr_lo]
        wsh_gu, wsh_dn = w_sh_gu[l].astype(BF16), w_sh_down[l].astype(BF16)

        (qa, qi, qb, sm, ka, kb, ki, ga, gb, kva_t, kvb_t, ki_t, vat, vbt) = _inproj(
            xp, mod_p[1], mod_p[0], map_p, g_pre_mix[l], cos_p, sin_p, seq_tiles, w_proj, tm=tm_p, seq_tiles=seq_tiles)
        oa = _dsa_prompt(qi, qa, sm, ki, ka, vat, b=b, s=s, k_top=min(TOPK_MAX, s // 4))
        ob = _sb_prompt(qb, kb, vbt, b=b, s=s)
        x1_p, h2_p, lg_p = _outproj(oa, ob, ga, gb, xp, mod_p[2], mod_p[4], mod_p[3], map_p,
                                    g_post_mix[l], g_pre_ffn[l], w_mix, tm=tm_p)
        seq_first = lambda v: v.reshape(b, 2, N_HEADS, HEAD_DIM, s).transpose(0, 4, 1, 2, 3)
        state_p.append((seq_first(kva_t), ki_t.transpose(0, 2, 1), seq_first(kvb_t)))

        (qa, qi, qb, sm, _, _, _, ga, gb, kva, kvb) = _inproj(
            xs, mod_s[1], mod_s[0], map_s, g_pre_mix[l], cos_s, sin_s, n_s // tm_s, w_proj, tm=tm_s, seq_tiles=None)
        oa = _dsa_sample(qi.reshape(db, t, WIDTH), qa.reshape(db, t, WIDTH), sm.reshape(db, t, LANES),
                         kva.reshape(db, t, 2 * WIDTH), cache_i, cache_a, page_table, l,
                         k_top=min(TOPK_MAX, (past + t) // 4))
        ob = _sb_sample(qb.reshape(db, t, WIDTH), kvb.reshape(db, t, 2 * WIDTH), cache_b, page_table, l)
        x1_s, h2_s, lg_s = _outproj(oa.reshape(n_s, WIDTH), ob.reshape(n_s, WIDTH), ga, gb, xs, mod_s[2], mod_s[4],
                                    mod_s[3], map_s, g_post_mix[l], g_pre_ffn[l], w_mix, tm=tm_s)
        state_s.append((kva.reshape(db, t, 2, N_HEADS, HEAD_DIM), sm[:, :IDX_DIM].reshape(db, t, IDX_DIM),
                        kvb.reshape(db, t, 2, N_HEADS, HEAD_DIM)))

        n_all = n_p + n_s
        br = MOE_BLOCK_ROWS
        n_blocks = -(-(n_all * EXPERT_TOP_K + n_experts * (br - 1)) // br)
        eidx, gw, rank, counts = _route(jnp.concatenate([lg_p, lg_s], axis=1), b_router[l], tn=tm_p)
        pad_start, block_expert, block_valid = _expert_layout(counts, n_blocks, br)
        dest = _moe_dest(eidx, rank, pad_start, tn=tm_p)
        x_sorted = _sc_scatter_rows(jnp.concatenate([h2_p, h2_s], axis=0), dest, n_blocks * br)
        y_sorted = _experts(x_sorted, block_expert, block_valid, w_exp_gu, w_exp_down, l)
        y_tok = _sc_gather_rows(y_sorted, dest.reshape(-1)).reshape(EXPERT_TOP_K, n_all, d // 2)
        gw_t = gw.T
        xp = _ffn_out(h2_p, y_tok, gw_t, x1_p, mod_p[5], map_p, g_post_ffn[l], wsh_gu, wsh_dn, tm=tm_p)
        xs = _ffn_out(h2_s, y_tok, gw_t, x1_s, mod_s[5], map_s, g_post_ffn[l], wsh_gu, wsh_dn, tm=tm_s, row0=n_p)

    stack = lambda states, k: jnp.stack([st[k] for st in states])
    return (xp.reshape(b, s, d), xs.reshape(db, t, d), stack(state_p, 0), stack(state_p, 1), stack(state_p, 2),
            stack(state_s, 0), stack(state_s, 1), stack(state_s, 2))
```

```python
import functools

import jax
import jax.numpy as jnp
from jax import lax
from jax.experimental import pallas as pl
from jax.experimental.pallas import tpu as pltpu
from jax.experimental.pallas import tpu_sc as plsc

F32 = jnp.float32
BF16 = jnp.bfloat16
I32 = jnp.int32

HEAD_DIM = 64
N_HEADS = 8
WIDTH = N_HEADS * HEAD_DIM
IDX_DIM = 64
N_IDX_HEADS = 8
TOPK_MAX = 256
ROPE_THETA = 10000.0
RMS_EPS = 1e-6
EXPERT_TOP_K = 8
N_EXPERT_GROUPS = 8
TOPK_GROUPS = 4
ROUTED_SCALE = 2.5

LANES = 128
ROW_TILE = 256
ATT_TILE = 256
PAGES_PER_STEP = 8
MOE_BLOCK_ROWS = 256
SC_WORKERS = 32
SC_WINDOW = 40
SC_INDEX_TILE = 128
VMEM_LIMIT = 56 * 1024 * 1024
INT_MIN = -(2 ** 31)
NEG_INF = float("-inf")


def _cparams(*sem):
    return pltpu.CompilerParams(dimension_semantics=sem, vmem_limit_bytes=VMEM_LIMIT)


def _dot(a, b):
    return jnp.dot(a, b, preferred_element_type=F32)


def _dot_nt(a, b):
    return lax.dot_general(a, b, (((1,), (1,)), ((), ())), preferred_element_type=F32)


def _split_bf16(v):
    hi = v.astype(BF16)
    lo = (v - hi.astype(F32)).astype(BF16)
    return hi, lo


def _pack_bf16_pairs(v):
    c = v.shape[1] // 2
    hi = lax.bitcast_convert_type(v[:, :c].astype(BF16).astype(F32), I32)
    lo = lax.bitcast_convert_type(v[:, c:].astype(BF16).astype(F32), I32)
    return hi | lax.shift_right_logical(lo, jnp.int32(16))


def _unpack_bf16_pairs(p):
    hi = lax.bitcast_convert_type(p & jnp.int32(-65536), F32)
    lo = lax.bitcast_convert_type(lax.shift_left(p, jnp.int32(16)), F32)
    return jnp.concatenate([hi, lo], axis=1)


def _rms(v, g):
    return v * lax.rsqrt(jnp.mean(v * v, axis=-1, keepdims=True) + RMS_EPS) * g


def _ada_kernel(c_ref, w_ref, b_ref, o_ref):
    c = c_ref[...]
    a_hi, a_lo = _split_bf16(c * jax.nn.sigmoid(c))
    w_hi, w_lo = _split_bf16(w_ref[...])
    o_ref[...] = _dot(a_hi, w_hi) + _dot(a_hi, w_lo) + _dot(a_lo, w_hi) + b_ref[...]


def _ada(c, w, b):
    m, d = c.shape
    n = w.shape[1]
    tn = 1024 if n % 1024 == 0 else n
    return pl.pallas_call(
        _ada_kernel,
        grid=(n // tn,),
        in_specs=[pl.BlockSpec((m, d), lambda j: (0, 0)),
                  pl.BlockSpec((d, tn), lambda j: (0, j)),
                  pl.BlockSpec((1, tn), lambda j: (0, j))],
        out_specs=pl.BlockSpec((m, tn), lambda j: (0, j)),
        out_shape=jax.ShapeDtypeStruct((m, n), F32),
        compiler_params=_cparams("arbitrary"),
        name="ada_mod",
    )(c, w, b.reshape(1, n))


def _rope_cols(v, cos, sin_signed):
    rows = v.shape[0]
    lane = lax.broadcasted_iota(I32, (rows, LANES), 1)
    first_half = (lane % HEAD_DIM) < (HEAD_DIM // 2)
    outs = []
    for c in range(v.shape[1] // LANES):
        ch = v[:, c * LANES:(c + 1) * LANES]
        partner = jnp.where(first_half, pltpu.roll(ch, LANES - HEAD_DIM // 2, 1), pltpu.roll(ch, HEAD_DIM // 2, 1))
        outs.append(ch * cos + partner * sin_signed)
    return outs[0] if len(outs) == 1 else jnp.concatenate(outs, axis=1)


def _inproj_kernel(x_ref, sc_ref, sh_ref, g_ref, cos_ref, sin_ref,
                   wqa_ref, wkva_ref, wqi_ref, wsm_ref, wqb_ref, wkvb_ref, wg_ref,
                   qa_o, qi_o, qb_o, sm_o, ka_o, kb_o, ki_o, ga_o, gb_o, *state_o):
    d = x_ref.shape[1]
    h = _rms(x_ref[...], g_ref[...]) * (1.0 + sc_ref[0]) + sh_ref[0]
    hb = h.astype(BF16)
    cos = cos_ref[...]
    sin = sin_ref[...]

    qa_o[...] = _rope_cols(_dot(hb, wqa_ref[...]), cos, sin).astype(BF16)
    qi_o[...] = _rope_cols(_dot(hb, wqi_ref[...]), cos, sin).astype(BF16)
    qb_o[...] = _dot(hb, wqb_ref[...]).astype(BF16)

    kva = _dot(hb, wkva_ref[...])
    kva = jnp.concatenate([_rope_cols(kva[:, :WIDTH], cos, sin), kva[:, WIDTH:]], axis=1)
    ka_o[...] = kva[:, :WIDTH].astype(BF16)
    kvb = _dot(hb, wkvb_ref[...])
    kb_o[...] = kvb[:, :WIDTH].astype(BF16)

    sm = _dot(hb, wsm_ref[...])
    lane = lax.broadcasted_iota(I32, sm.shape, 1)
    sm = jnp.where(lane < IDX_DIM, _rope_cols(sm, cos, sin), sm)
    sm_o[...] = sm
    ki_o[...] = sm.astype(BF16)

    g = jax.nn.sigmoid(_dot(hb, wg_ref[...]))
    ga_o[...] = g[:, :d]
    gb_o[...] = g[:, d:]

    if len(state_o) == 2:
        kva_o, kvb_o = state_o
        kva_o[...] = kva
        kvb_o[...] = kvb
    else:
        kvat_o, kvbt_o, kit_o, vat_o, vbt_o = state_o
        kva_t, kvb_t = kva.T, kvb.T
        kvat_o[0] = kva_t
        kvbt_o[0] = kvb_t
        kit_o[0] = sm.T[:IDX_DIM, :]
        vat_o[0, 0] = kva_t[WIDTH:, :].astype(BF16)
        vbt_o[0, 0] = kvb_t[WIDTH:, :].astype(BF16)


def _inproj(x, scale, shift, mod_map, g, cos, sin, tbl_blocks, ws, *, tm, seq_tiles):
    n, d = x.shape
    r = scale.shape[1]
    grid = (n // tm,)
    row = lambda i: (i, 0)
    const = lambda i: (0, 0)
    in_specs = [pl.BlockSpec((tm, d), row),
                pl.BlockSpec((1, r, d), mod_map),
                pl.BlockSpec((1, r, d), mod_map),
                pl.BlockSpec((1, d), const),
                pl.BlockSpec((tm, LANES), lambda i: (i % tbl_blocks, 0)),
                pl.BlockSpec((tm, LANES), lambda i: (i % tbl_blocks, 0))]
    in_specs += [pl.BlockSpec(w.shape, const) for w in ws]
    out_cols = [(WIDTH, BF16), (WIDTH, BF16), (WIDTH, BF16), (LANES, F32),
                (WIDTH, BF16), (WIDTH, BF16), (LANES, BF16), (d, F32), (d, F32)]
    if seq_tiles is None:
        out_cols += [(2 * WIDTH, F32), (2 * WIDTH, F32)]
    out_specs = [pl.BlockSpec((tm, c), row) for c, _ in out_cols]
    out_shape = [jax.ShapeDtypeStruct((n, c), t) for c, t in out_cols]
    if seq_tiles is not None:
        nb = n // (tm * seq_tiles)
        seq_map = lambda i: (i // seq_tiles, 0, i % seq_tiles)
        vt_map = lambda i: (i // seq_tiles, i % seq_tiles, 0, 0)
        out_specs += [pl.BlockSpec((1, 2 * WIDTH, tm), seq_map)] * 2 + [pl.BlockSpec((1, IDX_DIM, tm), seq_map)]
        out_shape += [jax.ShapeDtypeStruct((nb, 2 * WIDTH, tm * seq_tiles), F32)] * 2
        out_shape += [jax.ShapeDtypeStruct((nb, IDX_DIM, tm * seq_tiles), F32)]
        out_specs += [pl.BlockSpec((1, 1, WIDTH, tm), vt_map)] * 2
        out_shape += [jax.ShapeDtypeStruct((nb, seq_tiles, WIDTH, tm), BF16)] * 2
    return pl.pallas_call(
        _inproj_kernel, grid=grid, in_specs=in_specs, out_specs=out_specs, out_shape=out_shape,
        compiler_params=_cparams("arbitrary"), name="in_proj",
    )(x, scale, shift, g.reshape(1, d), cos, sin, *ws)


def _sortable(v):
    b = lax.bitcast_convert_type(v, I32)
    return b ^ ((b >> 31) & jnp.int32(0x7FFFFFFF))


def _select_topk(load_keys, n_chunks, part_sum, final_sum, pos_of_chunk, shape1, part_shape, k_top, idx_bits):
    def count(pred):
        def body(c, acc):
            return acc + part_sum(jnp.where(pred(load_keys(c), c), 1.0, 0.0))
        return final_sum(lax.fori_loop(0, n_chunks, body, jnp.zeros(part_shape, F32), unroll=isinstance(n_chunks, int)))

    def bit_body(bi, t_u):
        cand_u = t_u | jnp.left_shift(jnp.int32(1), 31 - bi)
        cand_s = cand_u ^ jnp.int32(INT_MIN)
        cnt = count(lambda k, c: k >= cand_s)
        return jnp.where(cnt >= k_top, cand_u, t_u)

    t_u = lax.fori_loop(0, 32, bit_body, jnp.zeros(shape1, I32))
    thr = t_u ^ jnp.int32(INT_MIN)
    need = k_top - count(lambda k, c: k > thr)

    def idx_body(bi, cut):
        cand = cut | jnp.left_shift(jnp.int32(1), idx_bits - 1 - bi)
        below = count(lambda k, c: (k == thr) & (pos_of_chunk(c) < cand))
        return jnp.where(below < need, cand, cut)

    n_ge = count(lambda k, c: (k >= thr) & (k != jnp.int32(INT_MIN)))
    split_ties = jnp.max(n_ge) > k_top
    cut = lax.cond(split_ties,
                   lambda: lax.fori_loop(0, idx_bits, idx_body, jnp.zeros(shape1, I32)),
                   lambda: jnp.full(shape1, (1 << idx_bits) - 1, I32))
    return thr, cut


def _dsa_prompt_kernel(qi_ref, qa_ref, sm_ref, ki_ref, ka_ref, vat_ref, o_ref, keys_scr, acc_scr, m_scr, l_scr,
                       *, k_top, idx_bits):
    t = qi_ref.shape[0]
    i = pl.program_id(1)
    n_chunks = i + 1
    q_pos = i * t + lax.broadcasted_iota(I32, (t, t), 1)
    k_iota = lax.broadcasted_iota(I32, (t, t), 0)

    qit = qi_ref[...].astype(F32).T.astype(BF16)
    smt = sm_ref[...].T
    idx_scale = IDX_DIM ** -0.5 * N_IDX_HEADS ** -0.5

    def score_chunk(c, carry):
        off = pl.multiple_of(c * t, t)
        kc = ki_ref[pl.ds(off, t), :][:, :IDX_DIM]
        acc = jnp.zeros((t, t), F32)
        for h in range(N_IDX_HEADS):
            s = _dot(kc, qit[h * IDX_DIM:(h + 1) * IDX_DIM, :])
            acc = acc + jnp.maximum(s, 0.0) * smt[IDX_DIM + h:IDX_DIM + h + 1, :]
        key = _sortable(acc * idx_scale)
        keys_scr[pl.ds(off, t), :] = jnp.where(c * t + k_iota <= q_pos, key, jnp.int32(INT_MIN))
        return carry

    lax.fori_loop(0, n_chunks, score_chunk, 0)

    def load_keys(c):
        return keys_scr[pl.ds(pl.multiple_of(c * t, t), t), :]

    thr, cut = _select_topk(load_keys, n_chunks, lambda x: x.reshape(t // 8, 8, t).sum(axis=0),
                            lambda p: jnp.sum(p, axis=0, keepdims=True), lambda c: c * t + k_iota,
                            (1, t), (8, t), k_top, idx_bits)

    qat = (qa_ref[...].astype(F32) * HEAD_DIM ** -0.5).T.astype(BF16)
    acc_scr[...] = jnp.zeros_like(acc_scr)
    m_scr[...] = jnp.full_like(m_scr, NEG_INF)
    l_scr[...] = jnp.zeros_like(l_scr)

    def att_chunk(c, carry):
        off = pl.multiple_of(c * t, t)
        key = keys_scr[pl.ds(off, t), :]
        sel = ((key > thr) | ((key == thr) & (c * t + k_iota <= cut))) & (key != jnp.int32(INT_MIN))
        k = ka_ref[pl.ds(off, t), :]
        heads = [slice(h * HEAD_DIM, (h + 1) * HEAD_DIM) for h in range(N_HEADS)]
        logits = [jnp.where(sel, _dot(k[:, hs], qat[hs, :]), NEG_INF) for hs in heads]
        m_old = m_scr[...]
        m_new = jnp.maximum(m_old, jnp.concatenate([jnp.max(lt, axis=0, keepdims=True) for lt in logits], axis=0))
        m_safe = jnp.where(m_new == NEG_INF, 0.0, m_new)
        alpha = jnp.exp(m_old - m_safe)
        p = [jnp.exp(lt - m_safe[h:h + 1, :]) for h, lt in enumerate(logits)]
        l_scr[...] = alpha * l_scr[...] + jnp.concatenate([jnp.sum(ph, axis=0, keepdims=True) for ph in p], axis=0)
        m_scr[...] = m_new
        pv = [_dot(vat_ref[c, hs, :], ph.astype(BF16)) for hs, ph in zip(heads, p)]
        for h, hs in enumerate(heads):
            acc_scr[hs, :] = alpha[h:h + 1, :] * acc_scr[hs, :] + pv[h]
        return carry

    lax.fori_loop(0, n_chunks, att_chunk, 0)

    for h in range(N_HEADS):
        hs = slice(h * HEAD_DIM, (h + 1) * HEAD_DIM)
        acc_scr[hs, :] = acc_scr[hs, :] * (1.0 / l_scr[h:h + 1, :])
    o_ref[...] = acc_scr[...].T.astype(BF16)


def _dsa_prompt(qi, qa, sm, ki, ka, vat, *, b, s, k_top):
    t = ATT_TILE
    nq = s // t
    qmap = lambda bi, i: (bi * nq + i, 0)
    kmap = lambda bi, i: (bi, 0)
    return pl.pallas_call(
        functools.partial(_dsa_prompt_kernel, k_top=k_top, idx_bits=max(1, (s - 1).bit_length())),
        grid=(b, nq),
        in_specs=[pl.BlockSpec((t, WIDTH), qmap), pl.BlockSpec((t, WIDTH), qmap), pl.BlockSpec((t, LANES), qmap),
                  pl.BlockSpec((s, LANES), kmap), pl.BlockSpec((s, WIDTH), kmap),
                  pl.BlockSpec((None, nq, WIDTH, t), lambda bi, i: (bi, 0, 0, 0))],
        out_specs=pl.BlockSpec((t, WIDTH), qmap),
        out_shape=jax.ShapeDtypeStruct((b * s, WIDTH), BF16),
        scratch_shapes=[pltpu.VMEM((s, t), I32), pltpu.VMEM((WIDTH, t), F32),
                        pltpu.VMEM((N_HEADS, t), F32), pltpu.VMEM((N_HEADS, t), F32)],
        compiler_params=_cparams("arbitrary", "arbitrary"),
        name="dsa_prompt",
    )(qi, qa, sm, ki, ka, vat)


def _log_sigmoid_pair(z):
    tail = jnp.log(1.0 + jnp.exp(-jnp.abs(z)))
    return jnp.minimum(z, 0.0) - tail, jnp.minimum(-z, 0.0) - tail


def _sb_prompt_kernel(qb_ref, kb_ref, vbt_ref, o_ref, acc_scr, carry_scr):
    t = qb_ref.shape[0]
    i = pl.program_id(1)
    q_pos = i * t + lax.broadcasted_iota(I32, (t, t), 1)
    k_iota = lax.broadcasted_iota(I32, (t, t), 0)
    later = (lax.broadcasted_iota(I32, (t, t), 1) > k_iota).astype(BF16)
    qt = (qb_ref[...].astype(F32) * HEAD_DIM ** -0.5).T.astype(BF16)
    acc_scr[...] = jnp.zeros_like(acc_scr)
    carry_scr[...] = jnp.zeros_like(carry_scr)

    heads = [slice(h * HEAD_DIM, (h + 1) * HEAD_DIM) for h in range(N_HEADS)]

    def chunk(c, mask):
        off = pl.multiple_of(c * t, t)
        k = kb_ref[pl.ds(off, t), :]
        z = [_dot(k[:, hs], qt[hs, :]) for hs in heads]
        pairs = [_log_sigmoid_pair(zh) for zh in z]
        log_keep = [lk if mask is None else jnp.where(mask, lk, 0.0) for _, lk in pairs]
        split = [_split_bf16(lk) for lk in log_keep]
        suffix = [_dot(later, hi) + _dot(later, lo) + carry_scr[h:h + 1, :] for h, (hi, lo) in enumerate(split)]
        w = [jnp.exp(lb + sf) for (lb, _), sf in zip(pairs, suffix)]
        if mask is not None:
            w = [jnp.where(mask, wh, 0.0) for wh in w]
        pv = [_dot(vbt_ref[c, hs, :], wh.astype(BF16)) for hs, wh in zip(heads, w)]
        for h, hs in enumerate(heads):
            acc_scr[hs, :] = acc_scr[hs, :] + pv[h]
            carry_scr[h:h + 1, :] = carry_scr[h:h + 1, :] + jnp.sum(log_keep[h], axis=0, keepdims=True)

    chunk(i, i * t + k_iota < q_pos)

    def earlier_chunk(j, carry):
        chunk(i - 1 - j, None)
        return carry

    lax.fori_loop(0, i, earlier_chunk, 0)
    o_ref[...] = acc_scr[...].T.astype(BF16)


def _sb_prompt(qb, kb, vbt, *, b, s):
    t = ATT_TILE
    nq = s // t
    qmap = lambda bi, i: (bi * nq + i, 0)
    return pl.pallas_call(
        _sb_prompt_kernel,
        grid=(b, nq),
        in_specs=[pl.BlockSpec((t, WIDTH), qmap), pl.BlockSpec((s, WIDTH), lambda bi, i: (bi, 0)),
                  pl.BlockSpec((None, nq, WIDTH, t), lambda bi, i: (bi, 0, 0, 0))],
        out_specs=pl.BlockSpec((t, WIDTH), qmap),
        out_shape=jax.ShapeDtypeStruct((b * s, WIDTH), BF16),
        scratch_shapes=[pltpu.VMEM((WIDTH, t), F32), pltpu.VMEM((N_HEADS, t), F32)],
        compiler_params=_cparams("arbitrary", "arbitrary"),
        name="sb_prompt",
    )(qb, kb, vbt)


def _rows_by_head(q):
    return jnp.concatenate([q[:, h * HEAD_DIM:(h + 1) * HEAD_DIM] for h in range(N_HEADS)], axis=0)


def _block_diag_rows(q):
    t = q.shape[0]
    rows = jnp.concatenate([q] * N_HEADS, axis=0)
    r = lax.broadcasted_iota(I32, rows.shape, 0) // t
    c = lax.broadcasted_iota(I32, rows.shape, 1) // HEAD_DIM
    return jnp.where(r == c, rows, 0.0)


def _diag_heads(acc, t):
    return jnp.concatenate([acc[h * t:(h + 1) * t, h * HEAD_DIM:(h + 1) * HEAD_DIM] for h in range(N_HEADS)], axis=1)


def _pad_rows(v, rows):
    return jnp.concatenate([v, jnp.zeros((rows - v.shape[0], v.shape[1]), v.dtype)], axis=0)


def _pages_kv_t(pages):
    k_t = jnp.concatenate([pg[:WIDTH, :] for pg in pages], axis=1).astype(BF16)
    v_t = jnp.concatenate([pg[WIDTH:, :] for pg in pages], axis=1).astype(BF16)
    return k_t, v_t


def _new_kv_t(kv_new, page):
    kv_t = _pad_rows(kv_new, page).T.astype(BF16)
    return kv_t[:WIDTH, :], kv_t[WIDTH:, :]


def _dsa_sample_select_kernel(pt_ref, *refs, k_top, idx_bits, n_steps):
    pages = refs[:PAGES_PER_STEP]
    qi_ref, sm_ref, bias_o, keys_scr = refs[PAGES_PER_STEP:]
    j = pl.program_id(1)
    t = qi_ref.shape[0]
    page = pages[0].shape[1]
    cw = PAGES_PER_STEP * page
    idx_scale = IDX_DIM ** -0.5 * N_IDX_HEADS ** -0.5
    q_rows = _rows_by_head(qi_ref[...].astype(F32)).astype(BF16)
    sm = sm_ref[...]
    w_cols = [jnp.broadcast_to(sm[:, IDX_DIM + h:IDX_DIM + h + 1], (t, page)) for h in range(N_IDX_HEADS)]

    def scores(k_t):
        s = _dot(q_rows, k_t)
        tiles = []
        for c in range(k_t.shape[1] // page):
            acc = jnp.zeros((t, page), F32)
            for h in range(N_IDX_HEADS):
                acc = acc + jnp.maximum(s[h * t:(h + 1) * t, c * page:(c + 1) * page], 0.0) * w_cols[h]
            tiles.append(acc * idx_scale)
        return tiles[0] if len(tiles) == 1 else jnp.concatenate(tiles, axis=1)

    keys_scr[j] = _sortable(scores(jnp.concatenate([pg[...] for pg in pages], axis=1).astype(BF16)))

    @pl.when(j == n_steps - 1)
    def _():
        k_new = _pad_rows(sm[:, :IDX_DIM], page).T.astype(BF16)
        key = _sortable(scores(k_new))
        u = lax.broadcasted_iota(I32, (t, page), 1)
        q = lax.broadcasted_iota(I32, (t, page), 0)
        keys_scr[n_steps, :, :page] = jnp.where(u <= q, key, jnp.int32(INT_MIN))
        keys_scr[n_steps, :, page:] = jnp.full((t, cw - page), INT_MIN, I32)

        lane = lax.broadcasted_iota(I32, (t, cw), 1)
        part = lambda x: functools.reduce(lambda a, b2: a + b2, [x[:, k * page:(k + 1) * page] for k in range(PAGES_PER_STEP)])
        final = lambda p: jnp.sum(p, axis=1, keepdims=True)
        thr, cut = _select_topk(lambda c: keys_scr[c], n_steps + 1, part, final, lambda c: c * cw + lane,
                                (t, 1), (t, page), k_top, idx_bits)

        for c in range(n_steps + 1):
            key = keys_scr[c]
            sel = ((key > thr) | ((key == thr) & (c * cw + lane <= cut))) & (key != jnp.int32(INT_MIN))
            bias_o[c] = jnp.where(sel, 0.0, NEG_INF)


def _dsa_sample_attend_kernel(pt_ref, *refs, n_steps):
    pages = refs[:PAGES_PER_STEP]
    qa_ref, kv_new_ref, bias_ref, bias_new_ref, o_ref, acc_scr, m_scr, l_scr = refs[PAGES_PER_STEP:]
    j = pl.program_id(1)
    t = qa_ref.shape[0]
    page = pages[0].shape[1]
    q_bd = _block_diag_rows(qa_ref[...].astype(F32) * HEAD_DIM ** -0.5).astype(BF16)

    @pl.when(j == 0)
    def _():
        acc_scr[...] = jnp.zeros_like(acc_scr)
        m_scr[...] = jnp.full_like(m_scr, NEG_INF)
        l_scr[...] = jnp.zeros_like(l_scr)

    def update(k_t, v_t, bias):
        logits = _dot(q_bd, k_t) + jnp.concatenate([bias] * N_HEADS, axis=0)
        m_old = m_scr[...]
        m_new = jnp.maximum(m_old, jnp.max(logits, axis=1, keepdims=True))
        m_safe = jnp.where(m_new == NEG_INF, 0.0, m_new)
        alpha = jnp.exp(m_old - m_safe)
        p = jnp.exp(logits - m_safe)
        l_scr[...] = alpha * l_scr[...] + jnp.sum(p, axis=1, keepdims=True)
        m_scr[...] = m_new
        acc_scr[...] = alpha * acc_scr[...] + _dot_nt(p.astype(BF16), v_t)

    update(*_pages_kv_t(pages), bias_ref[...])

    @pl.when(j == n_steps - 1)
    def _():
        update(*_new_kv_t(kv_new_ref[...], page), bias_new_ref[:, :page])
        o_ref[...] = _diag_heads(acc_scr[...] * (1.0 / l_scr[...]), t).astype(BF16)


def _page_specs(cache, layer, n_pages, reverse=False):
    page, width = cache.shape[2:]

    def spec(r):
        if reverse:
            return pl.BlockSpec((None, None, page, width),
                                lambda b, j, pt: (layer, pt[b, n_pages - 1 - (j * PAGES_PER_STEP + r)], 0, 0))
        return pl.BlockSpec((None, None, page, width), lambda b, j, pt: (layer, pt[b, j * PAGES_PER_STEP + r], 0, 0))
    return [spec(r) for r in range(PAGES_PER_STEP)]


def _dsa_sample(qi, qa, sm, kva_new, cache_idx, cache_kv, page_table, layer, *, k_top):
    db, t, _ = qi.shape
    n_pages = page_table.shape[1]
    n_steps = n_pages // PAGES_PER_STEP
    page = cache_idx.shape[3]
    cw = PAGES_PER_STEP * page
    per_b = lambda width: pl.BlockSpec((None, t, width), lambda b, j, pt: (b, 0, 0))
    n_keys = n_pages * page + t
    bias = pl.pallas_call(
        functools.partial(_dsa_sample_select_kernel, k_top=k_top, idx_bits=max(1, (n_keys - 1).bit_length()), n_steps=n_steps),
        grid_spec=pltpu.PrefetchScalarGridSpec(
            num_scalar_prefetch=1, grid=(db, n_steps),
            in_specs=_page_specs(cache_idx, layer, n_pages) + [per_b(WIDTH), per_b(LANES)],
            out_specs=pl.BlockSpec((None, n_steps + 1, t, cw), lambda b, j, pt: (b, 0, 0, 0)),
            scratch_shapes=[pltpu.VMEM((n_steps + 1, t, cw), I32)]),
        out_shape=jax.ShapeDtypeStruct((db, n_steps + 1, t, cw), F32),
        compiler_params=_cparams("arbitrary", "arbitrary"),
        name="dsa_sample_select",
    )(page_table, *([cache_idx] * PAGES_PER_STEP), qi, sm)
    return pl.pallas_call(
        functools.partial(_dsa_sample_attend_kernel, n_steps=n_steps),
        grid_spec=pltpu.PrefetchScalarGridSpec(
            num_scalar_prefetch=1, grid=(db, n_steps),
            in_specs=_page_specs(cache_kv, layer, n_pages) + [
                per_b(WIDTH), per_b(2 * WIDTH),
                pl.BlockSpec((None, None, t, cw), lambda b, j, pt: (b, j, 0, 0)),
                pl.BlockSpec((None, None, t, cw), lambda b, j, pt: (b, n_steps, 0, 0))],
            out_specs=per_b(WIDTH),
            scratch_shapes=[pltpu.VMEM((N_HEADS * t, WIDTH), F32), pltpu.VMEM((N_HEADS * t, 1), F32),
                            pltpu.VMEM((N_HEADS * t, 1), F32)]),
        out_shape=jax.ShapeDtypeStruct((db, t, WIDTH), BF16),
        compiler_params=_cparams("arbitrary", "arbitrary"),
        name="dsa_sample_attend",
    )(page_table, *([cache_kv] * PAGES_PER_STEP), qa, kva_new, bias, bias)


def _suffix_scan_lanes(x):
    lane = lax.broadcasted_iota(I32, x.shape, 1)
    sh = 1
    while sh < LANES:
        x = x + jnp.where(lane + sh < LANES, pltpu.roll(x, LANES - sh, 1), 0.0)
        sh *= 2
    return x


def _sb_sample_kernel(pt_ref, *refs, n_steps):
    pages = refs[:PAGES_PER_STEP]
    qb_ref, kv_new_ref, o_ref, acc_scr, carry_scr = refs[PAGES_PER_STEP:]
    j = pl.program_id(1)
    t = qb_ref.shape[0]
    page = pages[0].shape[1]
    q_bd = _block_diag_rows(qb_ref[...].astype(F32) * HEAD_DIM ** -0.5).astype(BF16)

    def weights(z, mask):
        outs = []
        carry = carry_scr[...]
        for k in range(z.shape[1] // page):
            zt = z[:, k * page:(k + 1) * page]
            log_beta, log_keep = _log_sigmoid_pair(zt)
            if mask is not None:
                log_keep = jnp.where(mask, log_keep, 0.0)
            incl = _suffix_scan_lanes(log_keep)
            w = jnp.exp(log_beta + (incl - log_keep) + carry)
            if mask is not None:
                w = jnp.where(mask, w, 0.0)
            outs.append(w)
            carry = carry + incl[:, :1]
        carry_scr[...] = carry
        return outs[0] if len(outs) == 1 else jnp.concatenate(outs, axis=1)

    @pl.when(j == 0)
    def _():
        carry_scr[...] = jnp.zeros_like(carry_scr)
        k_t, v_t = _new_kv_t(kv_new_ref[...], page)
        u = lax.broadcasted_iota(I32, (N_HEADS * t, page), 1)
        q = lax.broadcasted_iota(I32, (N_HEADS * t, page), 0) % t
        w = weights(_dot(q_bd, k_t), u < q)
        acc_scr[...] = _dot_nt(w.astype(BF16), v_t)

    k_t, v_t = _pages_kv_t(pages)
    w = weights(_dot(q_bd, k_t), None)
    acc_scr[...] = acc_scr[...] + _dot_nt(w.astype(BF16), v_t)

    @pl.when(j == n_steps - 1)
    def _():
        o_ref[...] = _diag_heads(acc_scr[...], t).astype(BF16)


def _sb_sample(qb, kvb_new, cache_kv, page_table, layer):
    db, t, _ = qb.shape
    n_pages = page_table.shape[1]
    n_steps = n_pages // PAGES_PER_STEP
    per_b = lambda width: pl.BlockSpec((None, t, width), lambda b, j, pt: (b, 0, 0))
    return pl.pallas_call(
        functools.partial(_sb_sample_kernel, n_steps=n_steps),
        grid_spec=pltpu.PrefetchScalarGridSpec(
            num_scalar_prefetch=1, grid=(db, n_steps),
            in_specs=_page_specs(cache_kv, layer, n_pages, reverse=True) + [per_b(WIDTH), per_b(2 * WIDTH)],
            out_specs=per_b(WIDTH),
            scratch_shapes=[pltpu.VMEM((N_HEADS * t, WIDTH), F32), pltpu.VMEM((N_HEADS * t, 1), F32)]),
        out_shape=jax.ShapeDtypeStruct((db, t, WIDTH), BF16),
        compiler_params=_cparams("arbitrary", "arbitrary"),
        name="sb_sample",
    )(page_table, *([cache_kv] * PAGES_PER_STEP), qb, kvb_new)


def _outproj_kernel(oa_ref, ob_ref, ga_ref, gb_ref, x_ref, gate_ref, sc_ref, sh_ref, gpost_ref, gpre_ref,
                    wpa_ref, wpb_ref, wout_ref, wrh_ref, wrl_ref, x1_o, h2_o, lg_o):
    mix = ga_ref[...] * _dot(oa_ref[...], wpa_ref[...]) + gb_ref[...] * _dot(ob_ref[...], wpb_ref[...])
    mo = _dot(mix.astype(BF16), wout_ref[...])
    x1 = x_ref[...] + gate_ref[0] * _rms(mo, gpost_ref[...])
    x1_o[...] = x1
    h2 = _rms(x1, gpre_ref[...]) * (1.0 + sc_ref[0]) + sh_ref[0]
    h_hi, h_lo = _split_bf16(h2)
    h2_o[...] = _pack_bf16_pairs(h2)
    lg_o[...] = _dot_nt(wrh_ref[...], h_hi) + _dot_nt(wrh_ref[...], h_lo) + _dot_nt(wrl_ref[...], h_hi)


def _outproj(oa, ob, ga, gb, x, gate, scale, shift, mod_map, gpost, gpre, ws, *, tm):
    n, d = x.shape
    r = gate.shape[1]
    e = ws[3].shape[0]
    row = lambda i: (i, 0)
    const = lambda i: (0, 0)
    in_specs = [pl.BlockSpec((tm, WIDTH), row), pl.BlockSpec((tm, WIDTH), row),
                pl.BlockSpec((tm, d), row), pl.BlockSpec((tm, d), row), pl.BlockSpec((tm, d), row),
                pl.BlockSpec((1, r, d), mod_map), pl.BlockSpec((1, r, d), mod_map), pl.BlockSpec((1, r, d), mod_map),
                pl.BlockSpec((1, d), const), pl.BlockSpec((1, d), const)]
    in_specs += [pl.BlockSpec(w.shape, const) for w in ws]
    return pl.pallas_call(
        _outproj_kernel, grid=(n // tm,), in_specs=in_specs,
        out_specs=[pl.BlockSpec((tm, d), row), pl.BlockSpec((tm, d // 2), row), pl.BlockSpec((e, tm), lambda i: (0, i))],
        out_shape=[jax.ShapeDtypeStruct((n, d), F32), jax.ShapeDtypeStruct((n, d // 2), I32),
                   jax.ShapeDtypeStruct((e, n), F32)],
        compiler_params=_cparams("arbitrary"), name="out_proj",
    )(oa, ob, ga, gb, x, gate, scale, shift, gpost.reshape(1, d), gpre.reshape(1, d), *ws)


def _route_kernel(lg_ref, b_ref, eidx_o, gw_o, rank_o, count_o, count_scr):
    @pl.when(pl.program_id(0) == 0)
    def _():
        count_scr[...] = jnp.zeros_like(count_scr)

    scores = jax.nn.sigmoid(lg_ref[...])
    biased = scores + b_ref[...]
    e, tn = scores.shape
    gsz = e // N_EXPERT_GROUPS
    group_score = []
    for g in range(N_EXPERT_GROUPS):
        xg = biased[g * gsz:(g + 1) * gsz, :]
        m1 = jnp.max(xg, axis=0, keepdims=True)
        n1 = jnp.sum(jnp.where(xg == m1, 1.0, 0.0), axis=0, keepdims=True)
        m2 = jnp.max(jnp.where(xg < m1, xg, NEG_INF), axis=0, keepdims=True)
        group_score.append(m1 + jnp.where(n1 >= 2.0, m1, m2))
    cur = []
    for g in range(N_EXPERT_GROUPS):
        rank = jnp.zeros((1, tn), F32)
        for g2 in range(N_EXPERT_GROUPS):
            if g2 == g:
                continue
            ahead = (group_score[g2] > group_score[g]) | ((group_score[g2] == group_score[g]) & (g2 < g))
            rank = rank + jnp.where(ahead, 1.0, 0.0)
        cur.append(jnp.where(rank < TOPK_GROUPS, biased[g * gsz:(g + 1) * gsz, :], NEG_INF))
    cur = jnp.concatenate(cur, axis=0)
    row = lax.broadcasted_iota(I32, (e, tn), 0).astype(F32)
    picked, hits = [], []
    for k in range(EXPERT_TOP_K):
        m = jnp.max(cur, axis=0, keepdims=True)
        idx = jnp.min(jnp.where(cur == m, row, float(e)), axis=0, keepdims=True)
        hit = row == idx
        hits.append(hit)
        eidx_o[k:k + 1, :] = idx.astype(I32)
        picked.append(jnp.sum(jnp.where(hit, scores, 0.0), axis=0, keepdims=True))
        cur = jnp.where(hit, NEG_INF, cur)
    total = functools.reduce(lambda a, b2: a + b2, picked)
    for k in range(EXPERT_TOP_K):
        gw_o[k:k + 1, :] = picked[k] / total * ROUTED_SCALE

    onehot = functools.reduce(lambda a, b2: a | b2, hits)
    earlier = lax.broadcasted_iota(I32, (tn, tn), 0) < lax.broadcasted_iota(I32, (tn, tn), 1)
    before = _dot(jnp.where(onehot, 1.0, 0.0).astype(BF16), earlier.astype(BF16)) + count_scr[...]
    for k in range(EXPERT_TOP_K):
        rank_o[k:k + 1, :] = jnp.sum(jnp.where(hits[k], before, 0.0), axis=0, keepdims=True).astype(I32)
    count_scr[...] = count_scr[...] + jnp.sum(jnp.where(onehot, 1.0, 0.0), axis=1, keepdims=True)
    count_o[...] = count_scr[...]


def _route(logits_t, b_router, *, tn):
    e, n = logits_t.shape
    tok = pl.BlockSpec((EXPERT_TOP_K, tn), lambda i: (0, i))
    return pl.pallas_call(
        _route_kernel, grid=(n // tn,),
        in_specs=[pl.BlockSpec((e, tn), lambda i: (0, i)), pl.BlockSpec((e, 1), lambda i: (0, 0))],
        out_specs=[tok, tok, tok, pl.BlockSpec((e, 1), lambda i: (0, 0))],
        out_shape=[jax.ShapeDtypeStruct((EXPERT_TOP_K, n), I32), jax.ShapeDtypeStruct((EXPERT_TOP_K, n), F32),
                   jax.ShapeDtypeStruct((EXPERT_TOP_K, n), I32), jax.ShapeDtypeStruct((e, 1), F32)],
        scratch_shapes=[pltpu.VMEM((e, 1), F32)],
        compiler_params=_cparams("arbitrary"), name="moe_route",
    )(logits_t, b_router.reshape(e, 1))


def _moe_dest_kernel(eidx_ref, rank_ref, start_ref, dest_o):
    e = start_ref.shape[0]
    tn = eidx_ref.shape[1]
    row = lax.broadcasted_iota(I32, (e, tn), 0)
    start = start_ref[...]
    for k in range(EXPERT_TOP_K):
        base = jnp.sum(jnp.where(row == eidx_ref[k:k + 1, :], start, 0.0), axis=0, keepdims=True)
        dest_o[k:k + 1, :] = base.astype(I32) + rank_ref[k:k + 1, :]


def _moe_dest(eidx, rank, pad_start, *, tn):
    k, n = eidx.shape
    e = pad_start.shape[0]
    tok = pl.BlockSpec((k, tn), lambda i: (0, i))
    return pl.pallas_call(
        _moe_dest_kernel, grid=(n // tn,),
        in_specs=[tok, tok, pl.BlockSpec((e, 1), lambda i: (0, 0))],
        out_specs=tok, out_shape=jax.ShapeDtypeStruct((k, n), I32),
        compiler_params=_cparams("arbitrary"), name="moe_dest",
    )(eidx, rank, pad_start)


def _experts_kernel(be_ref, bv_ref, first_ref, next_ref, slot_ref, x_ref, wgu_hbm, wdn_hbm, y_o,
                    wgu_buf, wdn_buf, wgu_scr, wdn_scr, sem, *, layer):
    i = pl.program_id(0)
    br = x_ref.shape[0]
    de = wdn_scr.shape[0]
    valid = bv_ref[i]

    def weight_copies(expert, slot):
        return (pltpu.make_async_copy(wgu_hbm.at[layer, expert], wgu_buf.at[slot], sem.at[0, slot]),
                pltpu.make_async_copy(wdn_hbm.at[layer, expert], wdn_buf.at[slot], sem.at[1, slot]))

    @pl.when(i == 0)
    def _():
        for c in weight_copies(be_ref[0], 0):
            c.start()

    @pl.when(first_ref[i] == 1)
    def _():
        slot = slot_ref[i]

        @pl.when(next_ref[i] >= 0)
        def _():
            for c in weight_copies(next_ref[i], 1 - slot):
                c.start()

        for c in weight_copies(be_ref[i], slot):
            c.wait()
        wgu_scr[...] = wgu_buf[slot].astype(BF16)
        wdn_scr[...] = wdn_buf[slot].astype(BF16)

    @pl.when(valid > 0)
    def _():
        live = lax.broadcasted_iota(I32, (br, 1), 0) < valid
        x = _unpack_bf16_pairs(jnp.where(live, x_ref[...], 0)).astype(BF16)
        gu = _dot(x, wgu_scr[...])
        gate = gu[:, :de]
        act = gate * jax.nn.sigmoid(gate) * gu[:, de:]
        y_o[...] = _pack_bf16_pairs(_dot(act.astype(BF16), wdn_scr[...]))

    @pl.when(valid == 0)
    def _():
        y_o[...] = jnp.zeros_like(y_o)


def _experts(x_sorted, block_expert, block_valid, w_gu, w_dn, layer):
    rows, dp = x_sorted.shape
    d = 2 * dp
    de = w_dn.shape[2]
    br = MOE_BLOCK_ROWS
    n_blocks = rows // br
    block = jnp.arange(n_blocks, dtype=I32)
    first = jnp.concatenate([jnp.ones((1,), I32), (block_expert[1:] != block_expert[:-1]).astype(I32)])
    slot = (jnp.cumsum(first) - 1) % 2
    next_first = jnp.flip(lax.cummin(jnp.flip(jnp.where(first == 1, block, n_blocks))))
    after = jnp.concatenate([next_first[1:], jnp.full((1,), n_blocks, I32)])
    next_expert = jnp.where(after < n_blocks, block_expert[jnp.minimum(after, n_blocks - 1)], -1)
    tile = lambda i, *_: (i, 0)
    return pl.pallas_call(
        functools.partial(_experts_kernel, layer=layer),
        grid_spec=pltpu.PrefetchScalarGridSpec(
            num_scalar_prefetch=5, grid=(n_blocks,),
            in_specs=[pl.BlockSpec((br, dp), tile),
                      pl.BlockSpec(memory_space=pl.ANY), pl.BlockSpec(memory_space=pl.ANY)],
            out_specs=pl.BlockSpec((br, dp), tile),
            scratch_shapes=[pltpu.VMEM((2, d, 2 * de), F32), pltpu.VMEM((2, de, d), F32),
                            pltpu.VMEM((d, 2 * de), BF16), pltpu.VMEM((de, d), BF16),
                            pltpu.SemaphoreType.DMA((2, 2))]),
        out_shape=jax.ShapeDtypeStruct((rows, dp), I32),
        compiler_params=_cparams("arbitrary"), name="moe_experts",
    )(block_expert, block_valid, first, next_expert.astype(I32), slot.astype(I32), x_sorted, w_gu, w_dn)


def _ffn_out_kernel(h2_ref, yg_ref, gw_ref, x1_ref, gate_ref, gpost_ref, wgu_ref, wdn_ref, y_o):
    ds = wdn_ref.shape[0]
    gu = _dot(_unpack_bf16_pairs(h2_ref[...]).astype(BF16), wgu_ref[...])
    gate = gu[:, :ds]
    act = gate * jax.nn.sigmoid(gate) * gu[:, ds:]
    f = _dot(act.astype(BF16), wdn_ref[...])
    gw = gw_ref[...]
    for k in range(EXPERT_TOP_K):
        f = f + _unpack_bf16_pairs(yg_ref[k]) * gw[:, k:k + 1]
    y_o[...] = x1_ref[...] + gate_ref[0] * _rms(f, gpost_ref[...])


def _ffn_out(h2, y_gathered, gw_t, x1, gate, mod_map, gpost, wgu, wdn, *, tm, row0=0):
    n, d = x1.shape
    r = gate.shape[1]
    k = y_gathered.shape[0]
    blk0 = row0 // tm
    row = lambda i: (i, 0)
    const = lambda i: (0, 0)
    return pl.pallas_call(
        _ffn_out_kernel, grid=(n // tm,),
        in_specs=[pl.BlockSpec((tm, d // 2), row), pl.BlockSpec((k, tm, d // 2), lambda i: (0, blk0 + i, 0)),
                  pl.BlockSpec((tm, k), lambda i: (blk0 + i, 0)), pl.BlockSpec((tm, d), row),
                  pl.BlockSpec((1, r, d), mod_map), pl.BlockSpec((1, d), const),
                  pl.BlockSpec(wgu.shape, const), pl.BlockSpec(wdn.shape, const)],
        out_specs=pl.BlockSpec((tm, d), row),
        out_shape=jax.ShapeDtypeStruct((n, d), F32),
        compiler_params=_cparams("arbitrary"), name="ffn_out",
    )(h2, y_gathered, gw_t, x1, gate, gpost.reshape(1, d), wgu, wdn)


def _sc_index_rows(idx):
    steps = idx.shape[0] // SC_WINDOW
    assert idx.shape[0] % SC_WINDOW == 0 and steps % SC_WORKERS == 0
    return jnp.pad(idx.reshape(steps, SC_WINDOW), ((0, 0), (0, SC_INDEX_TILE - SC_WINDOW)))


def _sc_scatter_rows(src, dest, rows_out):
    n, d = src.shape
    k = dest.shape[0]
    steps = n // SC_WINDOW
    index_rows = _sc_index_rows(dest.reshape(-1))
    mesh = plsc.VectorSubcoreMesh(core_axis_name="core", subcore_axis_name="subcore")

    @pl.kernel(out_type=jax.ShapeDtypeStruct((rows_out, d), src.dtype), mesh=mesh, scratch_types=[])
    def scatter(src_hbm, idx_hbm, out_hbm):
        def step(src_vmem, idx_vmem):
            pltpu.sync_copy(src_vmem, out_hbm.at[idx_vmem.at[0, pl.ds(0, SC_WINDOW)]])

        pltpu.emit_pipeline(
            step, grid=(steps, k),
            in_specs=[pl.BlockSpec((SC_WINDOW, d), lambda i, kk: (i, 0)),
                      pl.BlockSpec((1, SC_INDEX_TILE), lambda i, kk: (kk * steps + i, 0))],
            out_specs=[],
            core_axis_name=("core", "subcore"),
            dimension_semantics=(pltpu.PARALLEL, pltpu.ARBITRARY),
        )(src_hbm, idx_hbm)

    return scatter(src, index_rows)


def _sc_gather_rows(src, idx):
    m = idx.shape[0]
    d = src.shape[1]
    index_rows = _sc_index_rows(idx)
    mesh = plsc.VectorSubcoreMesh(core_axis_name="core", subcore_axis_name="subcore")

    @pl.kernel(out_type=jax.ShapeDtypeStruct((m, d), src.dtype), mesh=mesh, scratch_types=[])
    def gather(src_hbm, idx_hbm, out_hbm):
        def step(idx_vmem, out_vmem):
            pltpu.sync_copy(src_hbm.at[idx_vmem.at[0, pl.ds(0, SC_WINDOW)]], out_vmem)

        pltpu.emit_pipeline(
            step, grid=(m // SC_WINDOW,),
            in_specs=[pl.BlockSpec((1, SC_INDEX_TILE), lambda i: (i, 0))],
            out_specs=[pl.BlockSpec((SC_WINDOW, d), lambda i: (i, 0))],
            core_axis_name=("core", "subcore"),
            dimension_semantics=(pltpu.PARALLEL,),
        )(idx_hbm, out_hbm)

    return gather(src, index_rows)


def _rope_tables(pos):
    half = HEAD_DIM // 2
    inv_freq = ROPE_THETA ** (-jnp.arange(half, dtype=F32) / half)
    ang = pos.astype(F32)[:, None] * inv_freq[None, :]
    cos, sin = jnp.cos(ang), jnp.sin(ang)
    reps = LANES // HEAD_DIM
    return (jnp.tile(jnp.concatenate([cos, cos], axis=1), (1, reps)),
            jnp.tile(jnp.concatenate([-sin, sin], axis=1), (1, reps)))


def _expert_layout(counts, n_blocks, br):
    n_experts = counts.shape[0]
    count = counts[:, 0].astype(I32)
    padded = (count + br - 1) // br * br
    pad_end = jnp.cumsum(padded)
    pad_start = pad_end - padded
    block_start = jnp.arange(n_blocks, dtype=I32) * br
    block_expert = jnp.minimum(jnp.sum((pad_end[None, :] <= block_start[:, None]).astype(I32), axis=1), n_experts - 1)
    in_expert = (jnp.arange(n_experts, dtype=I32)[None, :] == block_expert[:, None]) & (block_start[:, None] < pad_end[None, :])
    live_end = jnp.sum(jnp.where(in_expert, (pad_start + count)[None, :], 0), axis=1)
    block_valid = jnp.clip(live_end - block_start, 0, br).astype(I32)
    return pad_start.astype(F32)[:, None], block_expert, block_valid


def kernel(x_prompt, x_sample, c_prompt, c_sample, cache_a_kv, cache_a_idx, cache_b_kv, page_table, w_ada, b_ada,
           g_pre_mix, g_post_mix, g_pre_ffn, g_post_ffn, w_in, w_proj_a, w_proj_b, w_out, w_router, b_router,
           w_exp_gu, w_exp_down, w_sh_gu, w_sh_down):
    b, s, d = x_prompt.shape
    db, t, _ = x_sample.shape
    n_p, n_s = b * s, db * t
    depth = w_ada.shape[0]
    n_layers, n_phys, page = cache_a_kv.shape[:3]
    past = page_table.shape[1] * page
    n_experts = w_router.shape[2]
    tm_p = ROW_TILE
    tm_s = min(ROW_TILE, n_s)
    assert s % ATT_TILE == 0 and ATT_TILE == tm_p and n_s % tm_s == 0 and (n_p + n_s) % tm_p == 0
    assert page_table.shape[1] % PAGES_PER_STEP == 0 and t <= page

    cos_p, sin_p = _rope_tables(jnp.arange(s, dtype=jnp.int32))
    cos_s, sin_s = _rope_tables(past + jnp.arange(t, dtype=jnp.int32))
    cos_s, sin_s = jnp.tile(cos_s, (db, 1)), jnp.tile(sin_s, (db, 1))
    cache_a = cache_a_kv.transpose(0, 1, 3, 4, 5, 2).reshape(n_layers, n_phys, 2 * WIDTH, page)
    cache_b = cache_b_kv.transpose(0, 1, 3, 4, 5, 2).reshape(n_layers, n_phys, 2 * WIDTH, page)
    cache_i = cache_a_idx.transpose(0, 1, 3, 2)
    seq_tiles = s // tm_p
    map_p = lambda i: (i // seq_tiles, 0, 0)
    map_s = lambda i: (0, i, 0)

    xp, xs = x_prompt.reshape(n_p, d), x_sample.reshape(n_s, d)
    state_p, state_s = [], []
    for l in range(depth):
        mod = _ada(jnp.concatenate([c_prompt, c_sample], axis=0), w_ada[l], b_ada[l]).reshape(b + db, 6, d)
        mod_p = [mod[:b, k][:, None, :] for k in range(6)]
        mod_s = [jnp.repeat(mod[b:, k], t, axis=0)[None] for k in range(6)]

        o = 0
        cols = []
        for width in (WIDTH, 2 * WIDTH, N_IDX_HEADS * IDX_DIM, IDX_DIM + N_IDX_HEADS, WIDTH, 2 * WIDTH, 2 * d):
            cols.append(w_in[l][:, o:o + width])
            o += width
        cols[3] = jnp.pad(cols[3], ((0, 0), (0, LANES - cols[3].shape[1])))
        w_proj = [c.astype(BF16) for c in (cols[0], cols[1], cols[2], cols[3], cols[4], cols[5], cols[6])]
        wr_hi = w_router[l].T.astype(BF16)
        wr_lo = (w_router[l].T - wr_hi.astype(F32)).astype(BF16)
        w_mix = [w_proj_a[l].astype(BF16), w_proj_b[l].astype(BF16), w_out[l].astype(BF16), wr_hi, wr_lo]
        wsh_gu, wsh_dn = w_sh_gu[l].astype(BF16), w_sh_down[l].astype(BF16)

        (qa, qi, qb, sm, ka, kb, ki, ga, gb, kva_t, kvb_t, ki_t, vat, vbt) = _inproj(
            xp, mod_p[1], mod_p[0], map_p, g_pre_mix[l], cos_p, sin_p, seq_tiles, w_proj, tm=tm_p, seq_tiles=seq_tiles)
        oa = _dsa_prompt(qi, qa, sm, ki, ka, vat, b=b, s=s, k_top=min(TOPK_MAX, s // 4))
        ob = _sb_prompt(qb, kb, vbt, b=b, s=s)
        x1_p, h2_p, lg_p = _outproj(oa, ob, ga, gb, xp, mod_p[2], mod_p[4], mod_p[3], map_p,
                                    g_post_mix[l], g_pre_ffn[l], w_mix, tm=tm_p)
        seq_first = lambda v: v.reshape(b, 2, N_HEADS, HEAD_DIM, s).transpose(0, 4, 1, 2, 3)
        state_p.append((seq_first(kva_t), ki_t.transpose(0, 2, 1), seq_first(kvb_t)))

        (qa, qi, qb, sm, _, _, _, ga, gb, kva, kvb) = _inproj(
            xs, mod_s[1], mod_s[0], map_s, g_pre_mix[l], cos_s, sin_s, n_s // tm_s, w_proj, tm=tm_s, seq_tiles=None)
        oa = _dsa_sample(qi.reshape(db, t, WIDTH), qa.reshape(db, t, WIDTH), sm.reshape(db, t, LANES),
                         kva.reshape(db, t, 2 * WIDTH), cache_i, cache_a, page_table, l,
                         k_top=min(TOPK_MAX, (past + t) // 4))
        ob = _sb_sample(qb.reshape(db, t, WIDTH), kvb.reshape(db, t, 2 * WIDTH), cache_b, page_table, l)
        x1_s, h2_s, lg_s = _outproj(oa.reshape(n_s, WIDTH), ob.reshape(n_s, WIDTH), ga, gb, xs, mod_s[2], mod_s[4],
                                    mod_s[3], map_s, g_post_mix[l], g_pre_ffn[l], w_mix, tm=tm_s)
        state_s.append((kva.reshape(db, t, 2, N_HEADS, HEAD_DIM), sm[:, :IDX_DIM].reshape(db, t, IDX_DIM),
                        kvb.reshape(db, t, 2, N_HEADS, HEAD_DIM)))

        n_all = n_p + n_s
        br = MOE_BLOCK_ROWS
        n_blocks = -(-(n_all * EXPERT_TOP_K + n_experts * (br - 1)) // br)
        eidx, gw, rank, counts = _route(jnp.concatenate([lg_p, lg_s], axis=1), b_router[l], tn=tm_p)
        pad_start, block_expert, block_valid = _expert_layout(counts, n_blocks, br)
        dest = _moe_dest(eidx, rank, pad_start, tn=tm_p)
        x_sorted = _sc_scatter_rows(jnp.concatenate([h2_p, h2_s], axis=0), dest, n_blocks * br)
        y_sorted = _experts(x_sorted, block_expert, block_valid, w_exp_gu, w_exp_down, l)
        y_tok = _sc_gather_rows(y_sorted, dest.reshape(-1)).reshape(EXPERT_TOP_K, n_all, d // 2)
        gw_t = gw.T
        xp = _ffn_out(h2_p, y_tok, gw_t, x1_p, mod_p[5], map_p, g_post_ffn[l], wsh_gu, wsh_dn, tm=tm_p)
        xs = _ffn_out(h2_s, y_tok, gw_t, x1_s, mod_s[5], map_s, g_post_ffn[l], wsh_gu, wsh_dn, tm=tm_s, row0=n_p)

    stack = lambda states, k: jnp.stack([st[k] for st in states])
    return (xp.reshape(b, s, d), xs.reshape(db, t, d), stack(state_p, 0), stack(state_p, 1), stack(state_p, 2),
            stack(state_s, 0), stack(state_s, 1), stack(state_s, 2))
```

```python
import functools

import jax
import jax.numpy as jnp
from jax import lax
from jax.experimental import pallas as pl
from jax.experimental.pallas import tpu as pltpu
from jax.experimental.pallas import tpu_sc as plsc

F32 = jnp.float32
BF16 = jnp.bfloat16
I32 = jnp.int32

HEAD_DIM = 64
N_HEADS = 8
WIDTH = N_HEADS * HEAD_DIM
IDX_DIM = 64
N_IDX_HEADS = 8
TOPK_MAX = 256
ROPE_THETA = 10000.0
RMS_EPS = 1e-6
EXPERT_TOP_K = 8
N_EXPERT_GROUPS = 8
TOPK_GROUPS = 4
ROUTED_SCALE = 2.5

LANES = 128
ROW_TILE = 256
ATT_TILE = 256
PAGES_PER_STEP = 8
MOE_BLOCK_ROWS = 256
SC_WORKERS = 32
SC_WINDOW = 104
SC_INDEX_TILE = 128
VMEM_LIMIT = 56 * 1024 * 1024
INT_MIN = -(2 ** 31)
NEG_INF = float("-inf")


def _cparams(*sem):
    return pltpu.CompilerParams(dimension_semantics=sem, vmem_limit_bytes=VMEM_LIMIT)


def _dot(a, b):
    return jnp.dot(a, b, preferred_element_type=F32)


def _dot_nt(a, b):
    return lax.dot_general(a, b, (((1,), (1,)), ((), ())), preferred_element_type=F32)


def _split_bf16(v):
    hi = v.astype(BF16)
    lo = (v - hi.astype(F32)).astype(BF16)
    return hi, lo


def _pack_bf16_pairs(v):
    c = v.shape[1] // 2
    hi = lax.bitcast_convert_type(v[:, :c].astype(BF16).astype(F32), I32)
    lo = lax.bitcast_convert_type(v[:, c:].astype(BF16).astype(F32), I32)
    return hi | lax.shift_right_logical(lo, jnp.int32(16))


def _unpack_bf16_pairs(p):
    hi = lax.bitcast_convert_type(p & jnp.int32(-65536), F32)
    lo = lax.bitcast_convert_type(lax.shift_left(p, jnp.int32(16)), F32)
    return jnp.concatenate([hi, lo], axis=1)


def _rms(v, g):
    return v * lax.rsqrt(jnp.mean(v * v, axis=-1, keepdims=True) + RMS_EPS) * g


def _ada_kernel(c_ref, w_ref, b_ref, o_ref):
    c = c_ref[...]
    a_hi, a_lo = _split_bf16(c * jax.nn.sigmoid(c))
    w_hi, w_lo = _split_bf16(w_ref[...])
    o_ref[...] = _dot(a_hi, w_hi) + _dot(a_hi, w_lo) + _dot(a_lo, w_hi) + b_ref[...]


def _ada(c, w, b):
    m, d = c.shape
    n = w.shape[1]
    tn = 1024 if n % 1024 == 0 else n
    return pl.pallas_call(
        _ada_kernel,
        grid=(n // tn,),
        in_specs=[pl.BlockSpec((m, d), lambda j: (0, 0)),
                  pl.BlockSpec((d, tn), lambda j: (0, j)),
                  pl.BlockSpec((1, tn), lambda j: (0, j))],
        out_specs=pl.BlockSpec((m, tn), lambda j: (0, j)),
        out_shape=jax.ShapeDtypeStruct((m, n), F32),
        compiler_params=_cparams("arbitrary"),
        name="ada_mod",
    )(c, w, b.reshape(1, n))


def _rope_cols(v, cos, sin_signed):
    rows = v.shape[0]
    lane = lax.broadcasted_iota(I32, (rows, LANES), 1)
    first_half = (lane % HEAD_DIM) < (HEAD_DIM // 2)
    outs = []
    for c in range(v.shape[1] // LANES):
        ch = v[:, c * LANES:(c + 1) * LANES]
        partner = jnp.where(first_half, pltpu.roll(ch, LANES - HEAD_DIM // 2, 1), pltpu.roll(ch, HEAD_DIM // 2, 1))
        outs.append(ch * cos + partner * sin_signed)
    return outs[0] if len(outs) == 1 else jnp.concatenate(outs, axis=1)


def _inproj_kernel(x_ref, sc_ref, sh_ref, g_ref, cos_ref, sin_ref,
                   wqa_ref, wkva_ref, wqi_ref, wsm_ref, wqb_ref, wkvb_ref, wg_ref,
                   qa_o, qi_o, qb_o, sm_o, ka_o, kb_o, ki_o, ga_o, gb_o, *state_o):
    d = x_ref.shape[1]
    h = _rms(x_ref[...], g_ref[...]) * (1.0 + sc_ref[0]) + sh_ref[0]
    hb = h.astype(BF16)
    cos = cos_ref[...]
    sin = sin_ref[...]

    qa_o[...] = _rope_cols(_dot(hb, wqa_ref[...]), cos, sin).astype(BF16)
    qi_o[...] = _rope_cols(_dot(hb, wqi_ref[...]), cos, sin).astype(BF16)
    qb_o[...] = _dot(hb, wqb_ref[...]).astype(BF16)

    kva = _dot(hb, wkva_ref[...])
    kva = jnp.concatenate([_rope_cols(kva[:, :WIDTH], cos, sin), kva[:, WIDTH:]], axis=1)
    ka_o[...] = kva[:, :WIDTH].astype(BF16)
    kvb = _dot(hb, wkvb_ref[...])
    kb_o[...] = kvb[:, :WIDTH].astype(BF16)

    sm = _dot(hb, wsm_ref[...])
    lane = lax.broadcasted_iota(I32, sm.shape, 1)
    sm = jnp.where(lane < IDX_DIM, _rope_cols(sm, cos, sin), sm)
    sm_o[...] = sm
    ki_o[...] = sm.astype(BF16)

    g = jax.nn.sigmoid(_dot(hb, wg_ref[...]))
    ga_o[...] = g[:, :d]
    gb_o[...] = g[:, d:]

    if len(state_o) == 2:
        kva_o, kvb_o = state_o
        kva_o[...] = kva
        kvb_o[...] = kvb
    else:
        kvat_o, kvbt_o, kit_o, vat_o, vbt_o = state_o
        kva_t, kvb_t = kva.T, kvb.T
        kvat_o[0] = kva_t
        kvbt_o[0] = kvb_t
        kit_o[0] = sm.T[:IDX_DIM, :]
        vat_o[0, 0] = kva_t[WIDTH:, :].astype(BF16)
        vbt_o[0, 0] = kvb_t[WIDTH:, :].astype(BF16)


def _inproj(x, scale, shift, mod_map, g, cos, sin, tbl_blocks, ws, *, tm, seq_tiles):
    n, d = x.shape
    r = scale.shape[1]
    grid = (n // tm,)
    row = lambda i: (i, 0)
    const = lambda i: (0, 0)
    in_specs = [pl.BlockSpec((tm, d), row),
                pl.BlockSpec((1, r, d), mod_map),
                pl.BlockSpec((1, r, d), mod_map),
                pl.BlockSpec((1, d), const),
                pl.BlockSpec((tm, LANES), lambda i: (i % tbl_blocks, 0)),
                pl.BlockSpec((tm, LANES), lambda i: (i % tbl_blocks, 0))]
    in_specs += [pl.BlockSpec(w.shape, const) for w in ws]
    out_cols = [(WIDTH, BF16), (WIDTH, BF16), (WIDTH, BF16), (LANES, F32),
                (WIDTH, BF16), (WIDTH, BF16), (LANES, BF16), (d, F32), (d, F32)]
    if seq_tiles is None:
        out_cols += [(2 * WIDTH, F32), (2 * WIDTH, F32)]
    out_specs = [pl.BlockSpec((tm, c), row) for c, _ in out_cols]
    out_shape = [jax.ShapeDtypeStruct((n, c), t) for c, t in out_cols]
    if seq_tiles is not None:
        nb = n // (tm * seq_tiles)
        seq_map = lambda i: (i // seq_tiles, 0, i % seq_tiles)
        vt_map = lambda i: (i // seq_tiles, i % seq_tiles, 0, 0)
        out_specs += [pl.BlockSpec((1, 2 * WIDTH, tm), seq_map)] * 2 + [pl.BlockSpec((1, IDX_DIM, tm), seq_map)]
        out_shape += [jax.ShapeDtypeStruct((nb, 2 * WIDTH, tm * seq_tiles), F32)] * 2
        out_shape += [jax.ShapeDtypeStruct((nb, IDX_DIM, tm * seq_tiles), F32)]
        out_specs += [pl.BlockSpec((1, 1, WIDTH, tm), vt_map)] * 2
        out_shape += [jax.ShapeDtypeStruct((nb, seq_tiles, WIDTH, tm), BF16)] * 2
    return pl.pallas_call(
        _inproj_kernel, grid=grid, in_specs=in_specs, out_specs=out_specs, out_shape=out_shape,
        compiler_params=_cparams("arbitrary"), name="in_proj",
    )(x, scale, shift, g.reshape(1, d), cos, sin, *ws)


def _sortable(v):
    b = lax.bitcast_convert_type(v, I32)
    return b ^ ((b >> 31) & jnp.int32(0x7FFFFFFF))


def _select_topk(load_keys, n_chunks, part_sum, final_sum, pos_of_chunk, shape1, part_shape, k_top, idx_bits):
    def count(pred):
        def body(c, acc):
            return acc + part_sum(jnp.where(pred(load_keys(c), c), 1.0, 0.0))
        return final_sum(lax.fori_loop(0, n_chunks, body, jnp.zeros(part_shape, F32), unroll=isinstance(n_chunks, int)))

    def bit_body(bi, t_u):
        cand_u = t_u | jnp.left_shift(jnp.int32(1), 31 - bi)
        cand_s = cand_u ^ jnp.int32(INT_MIN)
        cnt = count(lambda k, c: k >= cand_s)
        return jnp.where(cnt >= k_top, cand_u, t_u)

    t_u = lax.fori_loop(0, 32, bit_body, jnp.zeros(shape1, I32))
    thr = t_u ^ jnp.int32(INT_MIN)
    need = k_top - count(lambda k, c: k > thr)

    def idx_body(bi, cut):
        cand = cut | jnp.left_shift(jnp.int32(1), idx_bits - 1 - bi)
        below = count(lambda k, c: (k == thr) & (pos_of_chunk(c) < cand))
        return jnp.where(below < need, cand, cut)

    n_ge = count(lambda k, c: (k >= thr) & (k != jnp.int32(INT_MIN)))
    split_ties = jnp.max(n_ge) > k_top
    cut = lax.cond(split_ties,
                   lambda: lax.fori_loop(0, idx_bits, idx_body, jnp.zeros(shape1, I32)),
                   lambda: jnp.full(shape1, (1 << idx_bits) - 1, I32))
    return thr, cut


def _dsa_prompt_kernel(qi_ref, qa_ref, sm_ref, ki_ref, ka_ref, vat_ref, o_ref, keys_scr, acc_scr, m_scr, l_scr,
                       *, k_top, idx_bits):
    t = qi_ref.shape[0]
    i = pl.program_id(1)
    n_chunks = i + 1
    q_pos = i * t + lax.broadcasted_iota(I32, (t, t), 1)
    k_iota = lax.broadcasted_iota(I32, (t, t), 0)

    qit = qi_ref[...].astype(F32).T.astype(BF16)
    smt = sm_ref[...].T
    idx_scale = IDX_DIM ** -0.5 * N_IDX_HEADS ** -0.5

    def score_chunk(c, carry):
        off = pl.multiple_of(c * t, t)
        kc = ki_ref[pl.ds(off, t), :][:, :IDX_DIM]
        acc = jnp.zeros((t, t), F32)
        for h in range(N_IDX_HEADS):
            s = _dot(kc, qit[h * IDX_DIM:(h + 1) * IDX_DIM, :])
            acc = acc + jnp.maximum(s, 0.0) * smt[IDX_DIM + h:IDX_DIM + h + 1, :]
        key = _sortable(acc * idx_scale)
        keys_scr[pl.ds(off, t), :] = jnp.where(c * t + k_iota <= q_pos, key, jnp.int32(INT_MIN))
        return carry

    lax.fori_loop(0, n_chunks, score_chunk, 0)

    def load_keys(c):
        return keys_scr[pl.ds(pl.multiple_of(c * t, t), t), :]

    thr, cut = _select_topk(load_keys, n_chunks, lambda x: x.reshape(t // 8, 8, t).sum(axis=0),
                            lambda p: jnp.sum(p, axis=0, keepdims=True), lambda c: c * t + k_iota,
                            (1, t), (8, t), k_top, idx_bits)

    qat = (qa_ref[...].astype(F32) * HEAD_DIM ** -0.5).T.astype(BF16)
    acc_scr[...] = jnp.zeros_like(acc_scr)
    m_scr[...] = jnp.full_like(m_scr, NEG_INF)
    l_scr[...] = jnp.zeros_like(l_scr)

    def att_chunk(c, carry):
        off = pl.multiple_of(c * t, t)
        key = keys_scr[pl.ds(off, t), :]
        sel = ((key > thr) | ((key == thr) & (c * t + k_iota <= cut))) & (key != jnp.int32(INT_MIN))
        k = ka_ref[pl.ds(off, t), :]
        heads = [slice(h * HEAD_DIM, (h + 1) * HEAD_DIM) for h in range(N_HEADS)]
        logits = [jnp.where(sel, _dot(k[:, hs], qat[hs, :]), NEG_INF) for hs in heads]
        m_old = m_scr[...]
        m_new = jnp.maximum(m_old, jnp.concatenate([jnp.max(lt, axis=0, keepdims=True) for lt in logits], axis=0))
        m_safe = jnp.where(m_new == NEG_INF, 0.0, m_new)
        alpha = jnp.exp(m_old - m_safe)
        p = [jnp.exp(lt - m_safe[h:h + 1, :]) for h, lt in enumerate(logits)]
        l_scr[...] = alpha * l_scr[...] + jnp.concatenate([jnp.sum(ph, axis=0, keepdims=True) for ph in p], axis=0)
        m_scr[...] = m_new
        pv = [_dot(vat_ref[c, hs, :], ph.astype(BF16)) for hs, ph in zip(heads, p)]
        for h, hs in enumerate(heads):
            acc_scr[hs, :] = alpha[h:h + 1, :] * acc_scr[hs, :] + pv[h]
        return carry

    lax.fori_loop(0, n_chunks, att_chunk, 0)

    for h in range(N_HEADS):
        hs = slice(h * HEAD_DIM, (h + 1) * HEAD_DIM)
        acc_scr[hs, :] = acc_scr[hs, :] * (1.0 / l_scr[h:h + 1, :])
    o_ref[...] = acc_scr[...].T.astype(BF16)


def _dsa_prompt(qi, qa, sm, ki, ka, vat, *, b, s, k_top):
    t = ATT_TILE
    nq = s // t
    qmap = lambda bi, i: (bi * nq + i, 0)
    kmap = lambda bi, i: (bi, 0)
    return pl.pallas_call(
        functools.partial(_dsa_prompt_kernel, k_top=k_top, idx_bits=max(1, (s - 1).bit_length())),
        grid=(b, nq),
        in_specs=[pl.BlockSpec((t, WIDTH), qmap), pl.BlockSpec((t, WIDTH), qmap), pl.BlockSpec((t, LANES), qmap),
                  pl.BlockSpec((s, LANES), kmap), pl.BlockSpec((s, WIDTH), kmap),
                  pl.BlockSpec((None, nq, WIDTH, t), lambda bi, i: (bi, 0, 0, 0))],
        out_specs=pl.BlockSpec((t, WIDTH), qmap),
        out_shape=jax.ShapeDtypeStruct((b * s, WIDTH), BF16),
        scratch_shapes=[pltpu.VMEM((s, t), I32), pltpu.VMEM((WIDTH, t), F32),
                        pltpu.VMEM((N_HEADS, t), F32), pltpu.VMEM((N_HEADS, t), F32)],
        compiler_params=_cparams("arbitrary", "arbitrary"),
        name="dsa_prompt",
    )(qi, qa, sm, ki, ka, vat)


def _log_sigmoid_pair(z):
    tail = jnp.log(1.0 + jnp.exp(-jnp.abs(z)))
    return jnp.minimum(z, 0.0) - tail, jnp.minimum(-z, 0.0) - tail


def _sb_prompt_kernel(qb_ref, kb_ref, vbt_ref, o_ref, acc_scr, carry_scr):
    t = qb_ref.shape[0]
    i = pl.program_id(1)
    q_pos = i * t + lax.broadcasted_iota(I32, (t, t), 1)
    k_iota = lax.broadcasted_iota(I32, (t, t), 0)
    later = (lax.broadcasted_iota(I32, (t, t), 1) > k_iota).astype(BF16)
    qt = (qb_ref[...].astype(F32) * HEAD_DIM ** -0.5).T.astype(BF16)
    acc_scr[...] = jnp.zeros_like(acc_scr)
    carry_scr[...] = jnp.zeros_like(carry_scr)

    heads = [slice(h * HEAD_DIM, (h + 1) * HEAD_DIM) for h in range(N_HEADS)]

    def chunk(c, mask):
        off = pl.multiple_of(c * t, t)
        k = kb_ref[pl.ds(off, t), :]
        z = [_dot(k[:, hs], qt[hs, :]) for hs in heads]
        pairs = [_log_sigmoid_pair(zh) for zh in z]
        log_keep = [lk if mask is None else jnp.where(mask, lk, 0.0) for _, lk in pairs]
        split = [_split_bf16(lk) for lk in log_keep]
        suffix = [_dot(later, hi) + _dot(later, lo) + carry_scr[h:h + 1, :] for h, (hi, lo) in enumerate(split)]
        w = [jnp.exp(lb + sf) for (lb, _), sf in zip(pairs, suffix)]
        if mask is not None:
            w = [jnp.where(mask, wh, 0.0) for wh in w]
        pv = [_dot(vbt_ref[c, hs, :], wh.astype(BF16)) for hs, wh in zip(heads, w)]
        for h, hs in enumerate(heads):
            acc_scr[hs, :] = acc_scr[hs, :] + pv[h]
            carry_scr[h:h + 1, :] = carry_scr[h:h + 1, :] + jnp.sum(log_keep[h], axis=0, keepdims=True)

    chunk(i, i * t + k_iota < q_pos)

    def earlier_chunk(j, carry):
        chunk(i - 1 - j, None)
        return carry

    lax.fori_loop(0, i, earlier_chunk, 0)
    o_ref[...] = acc_scr[...].T.astype(BF16)


def _sb_prompt(qb, kb, vbt, *, b, s):
    t = ATT_TILE
    nq = s // t
    qmap = lambda bi, i: (bi * nq + i, 0)
    return pl.pallas_call(
        _sb_prompt_kernel,
        grid=(b, nq),
        in_specs=[pl.BlockSpec((t, WIDTH), qmap), pl.BlockSpec((s, WIDTH), lambda bi, i: (bi, 0)),
                  pl.BlockSpec((None, nq, WIDTH, t), lambda bi, i: (bi, 0, 0, 0))],
        out_specs=pl.BlockSpec((t, WIDTH), qmap),
        out_shape=jax.ShapeDtypeStruct((b * s, WIDTH), BF16),
        scratch_shapes=[pltpu.VMEM((WIDTH, t), F32), pltpu.VMEM((N_HEADS, t), F32)],
        compiler_params=_cparams("arbitrary", "arbitrary"),
        name="sb_prompt",
    )(qb, kb, vbt)


def _rows_by_head(q):
    return jnp.concatenate([q[:, h * HEAD_DIM:(h + 1) * HEAD_DIM] for h in range(N_HEADS)], axis=0)


def _block_diag_rows(q):
    t = q.shape[0]
    rows = jnp.concatenate([q] * N_HEADS, axis=0)
    r = lax.broadcasted_iota(I32, rows.shape, 0) // t
    c = lax.broadcasted_iota(I32, rows.shape, 1) // HEAD_DIM
    return jnp.where(r == c, rows, 0.0)


def _diag_heads(acc, t):
    return jnp.concatenate([acc[h * t:(h + 1) * t, h * HEAD_DIM:(h + 1) * HEAD_DIM] for h in range(N_HEADS)], axis=1)


def _pad_rows(v, rows):
    return jnp.concatenate([v, jnp.zeros((rows - v.shape[0], v.shape[1]), v.dtype)], axis=0)


def _pages_kv_t(pages):
    k_t = jnp.concatenate([pg[:WIDTH, :] for pg in pages], axis=1).astype(BF16)
    v_t = jnp.concatenate([pg[WIDTH:, :] for pg in pages], axis=1).astype(BF16)
    return k_t, v_t


def _new_kv_t(kv_new, page):
    kv_t = _pad_rows(kv_new, page).T.astype(BF16)
    return kv_t[:WIDTH, :], kv_t[WIDTH:, :]


def _dsa_sample_select_kernel(pt_ref, *refs, k_top, idx_bits, n_steps):
    pages = refs[:PAGES_PER_STEP]
    qi_ref, sm_ref, bias_o, keys_scr = refs[PAGES_PER_STEP:]
    j = pl.program_id(1)
    t = qi_ref.shape[0]
    page = pages[0].shape[1]
    cw = PAGES_PER_STEP * page
    idx_scale = IDX_DIM ** -0.5 * N_IDX_HEADS ** -0.5
    q_rows = _rows_by_head(qi_ref[...].astype(F32)).astype(BF16)
    sm = sm_ref[...]
    w_cols = [jnp.broadcast_to(sm[:, IDX_DIM + h:IDX_DIM + h + 1], (t, page)) for h in range(N_IDX_HEADS)]

    def scores(k_t):
        s = _dot(q_rows, k_t)
        tiles = []
        for c in range(k_t.shape[1] // page):
            acc = jnp.zeros((t, page), F32)
            for h in range(N_IDX_HEADS):
                acc = acc + jnp.maximum(s[h * t:(h + 1) * t, c * page:(c + 1) * page], 0.0) * w_cols[h]
            tiles.append(acc * idx_scale)
        return tiles[0] if len(tiles) == 1 else jnp.concatenate(tiles, axis=1)

    keys_scr[j] = _sortable(scores(jnp.concatenate([pg[...] for pg in pages], axis=1).astype(BF16)))

    @pl.when(j == n_steps - 1)
    def _():
        k_new = _pad_rows(sm[:, :IDX_DIM], page).T.astype(BF16)
        key = _sortable(scores(k_new))
        u = lax.broadcasted_iota(I32, (t, page), 1)
        q = lax.broadcasted_iota(I32, (t, page), 0)
        keys_scr[n_steps, :, :page] = jnp.where(u <= q, key, jnp.int32(INT_MIN))
        keys_scr[n_steps, :, page:] = jnp.full((t, cw - page), INT_MIN, I32)

        lane = lax.broadcasted_iota(I32, (t, cw), 1)
        part = lambda x: functools.reduce(lambda a, b2: a + b2, [x[:, k * page:(k + 1) * page] for k in range(PAGES_PER_STEP)])
        final = lambda p: jnp.sum(p, axis=1, keepdims=True)
        thr, cut = _select_topk(lambda c: keys_scr[c], n_steps + 1, part, final, lambda c: c * cw + lane,
                                (t, 1), (t, page), k_top, idx_bits)

        for c in range(n_steps + 1):
            key = keys_scr[c]
            sel = ((key > thr) | ((key == thr) & (c * cw + lane <= cut))) & (key != jnp.int32(INT_MIN))
            bias_o[c] = jnp.where(sel, 0.0, NEG_INF)


def _dsa_sample_attend_kernel(pt_ref, *refs, n_steps):
    pages = refs[:PAGES_PER_STEP]
    qa_ref, kv_new_ref, bias_ref, bias_new_ref, o_ref, acc_scr, m_scr, l_scr = refs[PAGES_PER_STEP:]
    j = pl.program_id(1)
    t = qa_ref.shape[0]
    page = pages[0].shape[1]
    q_bd = _block_diag_rows(qa_ref[...].astype(F32) * HEAD_DIM ** -0.5).astype(BF16)

    @pl.when(j == 0)
    def _():
        acc_scr[...] = jnp.zeros_like(acc_scr)
        m_scr[...] = jnp.full_like(m_scr, NEG_INF)
        l_scr[...] = jnp.zeros_like(l_scr)

    def update(k_t, v_t, bias):
        logits = _dot(q_bd, k_t) + jnp.concatenate([bias] * N_HEADS, axis=0)
        m_old = m_scr[...]
        m_new = jnp.maximum(m_old, jnp.max(logits, axis=1, keepdims=True))
        m_safe = jnp.where(m_new == NEG_INF, 0.0, m_new)
        alpha = jnp.exp(m_old - m_safe)
        p = jnp.exp(logits - m_safe)
        l_scr[...] = alpha * l_scr[...] + jnp.sum(p, axis=1, keepdims=True)
        m_scr[...] = m_new
        acc_scr[...] = alpha * acc_scr[...] + _dot_nt(p.astype(BF16), v_t)

    update(*_pages_kv_t(pages), bias_ref[...])

    @pl.when(j == n_steps - 1)
    def _():
        update(*_new_kv_t(kv_new_ref[...], page), bias_new_ref[:, :page])
        o_ref[...] = _diag_heads(acc_scr[...] * (1.0 / l_scr[...]), t).astype(BF16)


def _page_specs(cache, layer, n_pages, reverse=False):
    page, width = cache.shape[2:]

    def spec(r):
        if reverse:
            return pl.BlockSpec((None, None, page, width),
                                lambda b, j, pt: (layer, pt[b, n_pages - 1 - (j * PAGES_PER_STEP + r)], 0, 0))
        return pl.BlockSpec((None, None, page, width), lambda b, j, pt: (layer, pt[b, j * PAGES_PER_STEP + r], 0, 0))
    return [spec(r) for r in range(PAGES_PER_STEP)]


def _dsa_sample(qi, qa, sm, kva_new, cache_idx, cache_kv, page_table, layer, *, k_top):
    db, t, _ = qi.shape
    n_pages = page_table.shape[1]
    n_steps = n_pages // PAGES_PER_STEP
    page = cache_idx.shape[3]
    cw = PAGES_PER_STEP * page
    per_b = lambda width: pl.BlockSpec((None, t, width), lambda b, j, pt: (b, 0, 0))
    n_keys = n_pages * page + t
    bias = pl.pallas_call(
        functools.partial(_dsa_sample_select_kernel, k_top=k_top, idx_bits=max(1, (n_keys - 1).bit_length()), n_steps=n_steps),
        grid_spec=pltpu.PrefetchScalarGridSpec(
            num_scalar_prefetch=1, grid=(db, n_steps),
            in_specs=_page_specs(cache_idx, layer, n_pages) + [per_b(WIDTH), per_b(LANES)],
            out_specs=pl.BlockSpec((None, n_steps + 1, t, cw), lambda b, j, pt: (b, 0, 0, 0)),
            scratch_shapes=[pltpu.VMEM((n_steps + 1, t, cw), I32)]),
        out_shape=jax.ShapeDtypeStruct((db, n_steps + 1, t, cw), F32),
        compiler_params=_cparams("arbitrary", "arbitrary"),
        name="dsa_sample_select",
    )(page_table, *([cache_idx] * PAGES_PER_STEP), qi, sm)
    return pl.pallas_call(
        functools.partial(_dsa_sample_attend_kernel, n_steps=n_steps),
        grid_spec=pltpu.PrefetchScalarGridSpec(
            num_scalar_prefetch=1, grid=(db, n_steps),
            in_specs=_page_specs(cache_kv, layer, n_pages) + [
                per_b(WIDTH), per_b(2 * WIDTH),
                pl.BlockSpec((None, None, t, cw), lambda b, j, pt: (b, j, 0, 0)),
                pl.BlockSpec((None, None, t, cw), lambda b, j, pt: (b, n_steps, 0, 0))],
            out_specs=per_b(WIDTH),
            scratch_shapes=[pltpu.VMEM((N_HEADS * t, WIDTH), F32), pltpu.VMEM((N_HEADS * t, 1), F32),
                            pltpu.VMEM((N_HEADS * t, 1), F32)]),
        out_shape=jax.ShapeDtypeStruct((db, t, WIDTH), BF16),
        compiler_params=_cparams("arbitrary", "arbitrary"),
        name="dsa_sample_attend",
    )(page_table, *([cache_kv] * PAGES_PER_STEP), qa, kva_new, bias, bias)


def _suffix_scan_lanes(x):
    lane = lax.broadcasted_iota(I32, x.shape, 1)
    sh = 1
    while sh < LANES:
        x = x + jnp.where(lane + sh < LANES, pltpu.roll(x, LANES - sh, 1), 0.0)
        sh *= 2
    return x


def _sb_sample_kernel(pt_ref, *refs, n_steps):
    pages = refs[:PAGES_PER_STEP]
    qb_ref, kv_new_ref, o_ref, acc_scr, carry_scr = refs[PAGES_PER_STEP:]
    j = pl.program_id(1)
    t = qb_ref.shape[0]
    page = pages[0].shape[1]
    q_bd = _block_diag_rows(qb_ref[...].astype(F32) * HEAD_DIM ** -0.5).astype(BF16)

    def weights(z, mask):
        outs = []
        carry = carry_scr[...]
        for k in range(z.shape[1] // page):
            zt = z[:, k * page:(k + 1) * page]
            log_beta, log_keep = _log_sigmoid_pair(zt)
            if mask is not None:
                log_keep = jnp.where(mask, log_keep, 0.0)
            incl = _suffix_scan_lanes(log_keep)
            w = jnp.exp(log_beta + (incl - log_keep) + carry)
            if mask is not None:
                w = jnp.where(mask, w, 0.0)
            outs.append(w)
            carry = carry + incl[:, :1]
        carry_scr[...] = carry
        return outs[0] if len(outs) == 1 else jnp.concatenate(outs, axis=1)

    @pl.when(j == 0)
    def _():
        carry_scr[...] = jnp.zeros_like(carry_scr)
        k_t, v_t = _new_kv_t(kv_new_ref[...], page)
        u = lax.broadcasted_iota(I32, (N_HEADS * t, page), 1)
        q = lax.broadcasted_iota(I32, (N_HEADS * t, page), 0) % t
        w = weights(_dot(q_bd, k_t), u < q)
        acc_scr[...] = _dot_nt(w.astype(BF16), v_t)

    k_t, v_t = _pages_kv_t(pages)
    w = weights(_dot(q_bd, k_t), None)
    acc_scr[...] = acc_scr[...] + _dot_nt(w.astype(BF16), v_t)

    @pl.when(j == n_steps - 1)
    def _():
        o_ref[...] = _diag_heads(acc_scr[...], t).astype(BF16)


def _sb_sample(qb, kvb_new, cache_kv, page_table, layer):
    db, t, _ = qb.shape
    n_pages = page_table.shape[1]
    n_steps = n_pages // PAGES_PER_STEP
    per_b = lambda width: pl.BlockSpec((None, t, width), lambda b, j, pt: (b, 0, 0))
    return pl.pallas_call(
        functools.partial(_sb_sample_kernel, n_steps=n_steps),
        grid_spec=pltpu.PrefetchScalarGridSpec(
            num_scalar_prefetch=1, grid=(db, n_steps),
            in_specs=_page_specs(cache_kv, layer, n_pages, reverse=True) + [per_b(WIDTH), per_b(2 * WIDTH)],
            out_specs=per_b(WIDTH),
            scratch_shapes=[pltpu.VMEM((N_HEADS * t, WIDTH), F32), pltpu.VMEM((N_HEADS * t, 1), F32)]),
        out_shape=jax.ShapeDtypeStruct((db, t, WIDTH), BF16),
        compiler_params=_cparams("arbitrary", "arbitrary"),
        name="sb_sample",
    )(page_table, *([cache_kv] * PAGES_PER_STEP), qb, kvb_new)


def _outproj_kernel(oa_ref, ob_ref, ga_ref, gb_ref, x_ref, gate_ref, sc_ref, sh_ref, gpost_ref, gpre_ref,
                    wpa_ref, wpb_ref, wout_ref, wrh_ref, wrl_ref, x1_o, h2_o, lg_o):
    mix = ga_ref[...] * _dot(oa_ref[...], wpa_ref[...]) + gb_ref[...] * _dot(ob_ref[...], wpb_ref[...])
    mo = _dot(mix.astype(BF16), wout_ref[...])
    x1 = x_ref[...] + gate_ref[0] * _rms(mo, gpost_ref[...])
    x1_o[...] = x1
    h2 = _rms(x1, gpre_ref[...]) * (1.0 + sc_ref[0]) + sh_ref[0]
    h_hi, h_lo = _split_bf16(h2)
    h2_o[...] = _pack_bf16_pairs(h2)
    lg_o[...] = _dot_nt(wrh_ref[...], h_hi) + _dot_nt(wrh_ref[...], h_lo) + _dot_nt(wrl_ref[...], h_hi)


def _outproj(oa, ob, ga, gb, x, gate, scale, shift, mod_map, gpost, gpre, ws, *, tm):
    n, d = x.shape
    r = gate.shape[1]
    e = ws[3].shape[0]
    row = lambda i: (i, 0)
    const = lambda i: (0, 0)
    in_specs = [pl.BlockSpec((tm, WIDTH), row), pl.BlockSpec((tm, WIDTH), row),
                pl.BlockSpec((tm, d), row), pl.BlockSpec((tm, d), row), pl.BlockSpec((tm, d), row),
                pl.BlockSpec((1, r, d), mod_map), pl.BlockSpec((1, r, d), mod_map), pl.BlockSpec((1, r, d), mod_map),
                pl.BlockSpec((1, d), const), pl.BlockSpec((1, d), const)]
    in_specs += [pl.BlockSpec(w.shape, const) for w in ws]
    return pl.pallas_call(
        _outproj_kernel, grid=(n // tm,), in_specs=in_specs,
        out_specs=[pl.BlockSpec((tm, d), row), pl.BlockSpec((tm, d // 2), row), pl.BlockSpec((e, tm), lambda i: (0, i))],
        out_shape=[jax.ShapeDtypeStruct((n, d), F32), jax.ShapeDtypeStruct((n, d // 2), I32),
                   jax.ShapeDtypeStruct((e, n), F32)],
        compiler_params=_cparams("arbitrary"), name="out_proj",
    )(oa, ob, ga, gb, x, gate, scale, shift, gpost.reshape(1, d), gpre.reshape(1, d), *ws)


def _route_kernel(lg_ref, b_ref, eidx_o, gw_o, rank_o, count_o, count_scr):
    @pl.when(pl.program_id(0) == 0)
    def _():
        count_scr[...] = jnp.zeros_like(count_scr)

    scores = jax.nn.sigmoid(lg_ref[...])
    biased = scores + b_ref[...]
    e, tn = scores.shape
    gsz = e // N_EXPERT_GROUPS
    group_score = []
    for g in range(N_EXPERT_GROUPS):
        xg = biased[g * gsz:(g + 1) * gsz, :]
        m1 = jnp.max(xg, axis=0, keepdims=True)
        n1 = jnp.sum(jnp.where(xg == m1, 1.0, 0.0), axis=0, keepdims=True)
        m2 = jnp.max(jnp.where(xg < m1, xg, NEG_INF), axis=0, keepdims=True)
        group_score.append(m1 + jnp.where(n1 >= 2.0, m1, m2))
    cur = []
    for g in range(N_EXPERT_GROUPS):
        rank = jnp.zeros((1, tn), F32)
        for g2 in range(N_EXPERT_GROUPS):
            if g2 == g:
                continue
            ahead = (group_score[g2] > group_score[g]) | ((group_score[g2] == group_score[g]) & (g2 < g))
            rank = rank + jnp.where(ahead, 1.0, 0.0)
        cur.append(jnp.where(rank < TOPK_GROUPS, biased[g * gsz:(g + 1) * gsz, :], NEG_INF))
    cur = jnp.concatenate(cur, axis=0)
    row = lax.broadcasted_iota(I32, (e, tn), 0).astype(F32)
    picked, hits = [], []
    for k in range(EXPERT_TOP_K):
        m = jnp.max(cur, axis=0, keepdims=True)
        idx = jnp.min(jnp.where(cur == m, row, float(e)), axis=0, keepdims=True)
        hit = row == idx
        hits.append(hit)
        eidx_o[k:k + 1, :] = idx.astype(I32)
        picked.append(jnp.sum(jnp.where(hit, scores, 0.0), axis=0, keepdims=True))
        cur = jnp.where(hit, NEG_INF, cur)
    total = functools.reduce(lambda a, b2: a + b2, picked)
    for k in range(EXPERT_TOP_K):
        gw_o[k:k + 1, :] = picked[k] / total * ROUTED_SCALE

    onehot = functools.reduce(lambda a, b2: a | b2, hits)
    earlier = lax.broadcasted_iota(I32, (tn, tn), 0) < lax.broadcasted_iota(I32, (tn, tn), 1)
    before = _dot(jnp.where(onehot, 1.0, 0.0).astype(BF16), earlier.astype(BF16)) + count_scr[...]
    for k in range(EXPERT_TOP_K):
        rank_o[k:k + 1, :] = jnp.sum(jnp.where(hits[k], before, 0.0), axis=0, keepdims=True).astype(I32)
    count_scr[...] = count_scr[...] + jnp.sum(jnp.where(onehot, 1.0, 0.0), axis=1, keepdims=True)
    count_o[...] = count_scr[...]


def _route(logits_t, b_router, *, tn):
    e, n = logits_t.shape
    tok = pl.BlockSpec((EXPERT_TOP_K, tn), lambda i: (0, i))
    return pl.pallas_call(
        _route_kernel, grid=(n // tn,),
        in_specs=[pl.BlockSpec((e, tn), lambda i: (0, i)), pl.BlockSpec((e, 1), lambda i: (0, 0))],
        out_specs=[tok, tok, tok, pl.BlockSpec((e, 1), lambda i: (0, 0))],
        out_shape=[jax.ShapeDtypeStruct((EXPERT_TOP_K, n), I32), jax.ShapeDtypeStruct((EXPERT_TOP_K, n), F32),
                   jax.ShapeDtypeStruct((EXPERT_TOP_K, n), I32), jax.ShapeDtypeStruct((e, 1), F32)],
        scratch_shapes=[pltpu.VMEM((e, 1), F32)],
        compiler_params=_cparams("arbitrary"), name="moe_route",
    )(logits_t, b_router.reshape(e, 1))


def _moe_dest_kernel(eidx_ref, rank_ref, start_ref, dest_o):
    e = start_ref.shape[0]
    tn = eidx_ref.shape[1]
    row = lax.broadcasted_iota(I32, (e, tn), 0)
    start = start_ref[...]
    for k in range(EXPERT_TOP_K):
        base = jnp.sum(jnp.where(row == eidx_ref[k:k + 1, :], start, 0.0), axis=0, keepdims=True)
        dest_o[k:k + 1, :] = base.astype(I32) + rank_ref[k:k + 1, :]


def _moe_dest(eidx, rank, pad_start, *, tn):
    k, n = eidx.shape
    e = pad_start.shape[0]
    tok = pl.BlockSpec((k, tn), lambda i: (0, i))
    return pl.pallas_call(
        _moe_dest_kernel, grid=(n // tn,),
        in_specs=[tok, tok, pl.BlockSpec((e, 1), lambda i: (0, 0))],
        out_specs=tok, out_shape=jax.ShapeDtypeStruct((k, n), I32),
        compiler_params=_cparams("arbitrary"), name="moe_dest",
    )(eidx, rank, pad_start)


def _experts_kernel(be_ref, bv_ref, first_ref, next_ref, slot_ref, x_ref, wgu_hbm, wdn_hbm, y_o,
                    wgu_buf, wdn_buf, wgu_scr, wdn_scr, sem, *, layer):
    i = pl.program_id(0)
    br = x_ref.shape[0]
    de = wdn_scr.shape[0]
    valid = bv_ref[i]

    def weight_copies(expert, slot):
        return (pltpu.make_async_copy(wgu_hbm.at[layer, expert], wgu_buf.at[slot], sem.at[0, slot]),
                pltpu.make_async_copy(wdn_hbm.at[layer, expert], wdn_buf.at[slot], sem.at[1, slot]))

    @pl.when(i == 0)
    def _():
        for c in weight_copies(be_ref[0], 0):
            c.start()

    @pl.when(first_ref[i] == 1)
    def _():
        slot = slot_ref[i]

        @pl.when(next_ref[i] >= 0)
        def _():
            for c in weight_copies(next_ref[i], 1 - slot):
                c.start()

        for c in weight_copies(be_ref[i], slot):
            c.wait()
        wgu_scr[...] = wgu_buf[slot].astype(BF16)
        wdn_scr[...] = wdn_buf[slot].astype(BF16)

    @pl.when(valid > 0)
    def _():
        live = lax.broadcasted_iota(I32, (br, 1), 0) < valid
        x = _unpack_bf16_pairs(jnp.where(live, x_ref[...], 0)).astype(BF16)
        gu = _dot(x, wgu_scr[...])
        gate = gu[:, :de]
        act = gate * jax.nn.sigmoid(gate) * gu[:, de:]
        y_o[...] = _pack_bf16_pairs(_dot(act.astype(BF16), wdn_scr[...]))

    @pl.when(valid == 0)
    def _():
        y_o[...] = jnp.zeros_like(y_o)


def _experts(x_sorted, block_expert, block_valid, w_gu, w_dn, layer):
    rows, dp = x_sorted.shape
    d = 2 * dp
    de = w_dn.shape[2]
    br = MOE_BLOCK_ROWS
    n_blocks = rows // br
    block = jnp.arange(n_blocks, dtype=I32)
    first = jnp.concatenate([jnp.ones((1,), I32), (block_expert[1:] != block_expert[:-1]).astype(I32)])
    slot = (jnp.cumsum(first) - 1) % 2
    next_first = jnp.flip(lax.cummin(jnp.flip(jnp.where(first == 1, block, n_blocks))))
    after = jnp.concatenate([next_first[1:], jnp.full((1,), n_blocks, I32)])
    next_expert = jnp.where(after < n_blocks, block_expert[jnp.minimum(after, n_blocks - 1)], -1)
    tile = lambda i, *_: (i, 0)
    return pl.pallas_call(
        functools.partial(_experts_kernel, layer=layer),
        grid_spec=pltpu.PrefetchScalarGridSpec(
            num_scalar_prefetch=5, grid=(n_blocks,),
            in_specs=[pl.BlockSpec((br, dp), tile),
                      pl.BlockSpec(memory_space=pl.ANY), pl.BlockSpec(memory_space=pl.ANY)],
            out_specs=pl.BlockSpec((br, dp), tile),
            scratch_shapes=[pltpu.VMEM((2, d, 2 * de), F32), pltpu.VMEM((2, de, d), F32),
                            pltpu.VMEM((d, 2 * de), BF16), pltpu.VMEM((de, d), BF16),
                            pltpu.SemaphoreType.DMA((2, 2))]),
        out_shape=jax.ShapeDtypeStruct((rows, dp), I32),
        compiler_params=_cparams("arbitrary"), name="moe_experts",
    )(block_expert, block_valid, first, next_expert.astype(I32), slot.astype(I32), x_sorted, w_gu, w_dn)


def _ffn_out_kernel(h2_ref, yg_ref, gw_ref, x1_ref, gate_ref, gpost_ref, wgu_ref, wdn_ref, y_o):
    ds = wdn_ref.shape[0]
    gu = _dot(_unpack_bf16_pairs(h2_ref[...]).astype(BF16), wgu_ref[...])
    gate = gu[:, :ds]
    act = gate * jax.nn.sigmoid(gate) * gu[:, ds:]
    f = _dot(act.astype(BF16), wdn_ref[...])
    gw = gw_ref[...]
    for k in range(EXPERT_TOP_K):
        f = f + _unpack_bf16_pairs(yg_ref[k]) * gw[:, k:k + 1]
    y_o[...] = x1_ref[...] + gate_ref[0] * _rms(f, gpost_ref[...])


def _ffn_out(h2, y_gathered, gw_t, x1, gate, mod_map, gpost, wgu, wdn, *, tm, row0=0):
    n, d = x1.shape
    r = gate.shape[1]
    k = y_gathered.shape[0]
    blk0 = row0 // tm
    row = lambda i: (i, 0)
    const = lambda i: (0, 0)
    return pl.pallas_call(
        _ffn_out_kernel, grid=(n // tm,),
        in_specs=[pl.BlockSpec((tm, d // 2), row), pl.BlockSpec((k, tm, d // 2), lambda i: (0, blk0 + i, 0)),
                  pl.BlockSpec((tm, k), lambda i: (blk0 + i, 0)), pl.BlockSpec((tm, d), row),
                  pl.BlockSpec((1, r, d), mod_map), pl.BlockSpec((1, d), const),
                  pl.BlockSpec(wgu.shape, const), pl.BlockSpec(wdn.shape, const)],
        out_specs=pl.BlockSpec((tm, d), row),
        out_shape=jax.ShapeDtypeStruct((n, d), F32),
        compiler_params=_cparams("arbitrary"), name="ffn_out",
    )(h2, y_gathered, gw_t, x1, gate, gpost.reshape(1, d), wgu, wdn)


def _sc_index_rows(idx):
    steps = idx.shape[0] // SC_WINDOW
    assert idx.shape[0] % SC_WINDOW == 0 and steps % SC_WORKERS == 0
    return jnp.pad(idx.reshape(steps, SC_WINDOW), ((0, 0), (0, SC_INDEX_TILE - SC_WINDOW)))


def _sc_scatter_rows(src, dest, rows_out):
    n, d = src.shape
    k = dest.shape[0]
    steps = n // SC_WINDOW
    index_rows = _sc_index_rows(dest.reshape(-1))
    mesh = plsc.VectorSubcoreMesh(core_axis_name="core", subcore_axis_name="subcore")

    @pl.kernel(out_type=jax.ShapeDtypeStruct((rows_out, d), src.dtype), mesh=mesh, scratch_types=[])
    def scatter(src_hbm, idx_hbm, out_hbm):
        def step(src_vmem, idx_vmem):
            pltpu.sync_copy(src_vmem, out_hbm.at[idx_vmem.at[0, pl.ds(0, SC_WINDOW)]])

        pltpu.emit_pipeline(
            step, grid=(steps, k),
            in_specs=[pl.BlockSpec((SC_WINDOW, d), lambda i, kk: (i, 0)),
                      pl.BlockSpec((1, SC_INDEX_TILE), lambda i, kk: (kk * steps + i, 0))],
            out_specs=[],
            core_axis_name=("core", "subcore"),
            dimension_semantics=(pltpu.PARALLEL, pltpu.ARBITRARY),
        )(src_hbm, idx_hbm)

    return scatter(src, index_rows)


def _sc_gather_rows(src, idx):
    m = idx.shape[0]
    d = src.shape[1]
    index_rows = _sc_index_rows(idx)
    mesh = plsc.VectorSubcoreMesh(core_axis_name="core", subcore_axis_name="subcore")

    @pl.kernel(out_type=jax.ShapeDtypeStruct((m, d), src.dtype), mesh=mesh, scratch_types=[])
    def gather(src_hbm, idx_hbm, out_hbm):
        def step(idx_vmem, out_vmem):
            pltpu.sync_copy(src_hbm.at[idx_vmem.at[0, pl.ds(0, SC_WINDOW)]], out_vmem)

        pltpu.emit_pipeline(
            step, grid=(m // SC_WINDOW,),
            in_specs=[pl.BlockSpec((1, SC_INDEX_TILE), lambda i: (i, 0))],
            out_specs=[pl.BlockSpec((SC_WINDOW, d), lambda i: (i, 0))],
            core_axis_name=("core", "subcore"),
            dimension_semantics=(pltpu.PARALLEL,),
        )(idx_hbm, out_hbm)

    return gather(src, index_rows)


def _rope_tables(pos):
    half = HEAD_DIM // 2
    inv_freq = ROPE_THETA ** (-jnp.arange(half, dtype=F32) / half)
    ang = pos.astype(F32)[:, None] * inv_freq[None, :]
    cos, sin = jnp.cos(ang), jnp.sin(ang)
    reps = LANES // HEAD_DIM
    return (jnp.tile(jnp.concatenate([cos, cos], axis=1), (1, reps)),
            jnp.tile(jnp.concatenate([-sin, sin], axis=1), (1, reps)))


def _expert_layout(counts, n_blocks, br):
    n_experts = counts.shape[0]
    count = counts[:, 0].astype(I32)
    padded = (count + br - 1) // br * br
    pad_end = jnp.cumsum(padded)
    pad_start = pad_end - padded
    block_start = jnp.arange(n_blocks, dtype=I32) * br
    block_expert = jnp.minimum(jnp.sum((pad_end[None, :] <= block_start[:, None]).astype(I32), axis=1), n_experts - 1)
    in_expert = (jnp.arange(n_experts, dtype=I32)[None, :] == block_expert[:, None]) & (block_start[:, None] < pad_end[None, :])
    live_end = jnp.sum(jnp.where(in_expert, (pad_start + count)[None, :], 0), axis=1)
    block_valid = jnp.clip(live_end - block_start, 0, br).astype(I32)
    return pad_start.astype(F32)[:, None], block_expert, block_valid


def kernel(x_prompt, x_sample, c_prompt, c_sample, cache_a_kv, cache_a_idx, cache_b_kv, page_table, w_ada, b_ada,
           g_pre_mix, g_post_mix, g_pre_ffn, g_post_ffn, w_in, w_proj_a, w_proj_b, w_out, w_router, b_router,
           w_exp_gu, w_exp_down, w_sh_gu, w_sh_down):
    b, s, d = x_prompt.shape
    db, t, _ = x_sample.shape
    n_p, n_s = b * s, db * t
    depth = w_ada.shape[0]
    n_layers, n_phys, page = cache_a_kv.shape[:3]
    past = page_table.shape[1] * page
    n_experts = w_router.shape[2]
    tm_p = ROW_TILE
    tm_s = min(ROW_TILE, n_s)
    assert s % ATT_TILE == 0 and ATT_TILE == tm_p and n_s % tm_s == 0 and (n_p + n_s) % tm_p == 0
    assert page_table.shape[1] % PAGES_PER_STEP == 0 and t <= page

    cos_p, sin_p = _rope_tables(jnp.arange(s, dtype=jnp.int32))
    cos_s, sin_s = _rope_tables(past + jnp.arange(t, dtype=jnp.int32))
    cos_s, sin_s = jnp.tile(cos_s, (db, 1)), jnp.tile(sin_s, (db, 1))
    cache_a = cache_a_kv.transpose(0, 1, 3, 4, 5, 2).reshape(n_layers, n_phys, 2 * WIDTH, page)
    cache_b = cache_b_kv.transpose(0, 1, 3, 4, 5, 2).reshape(n_layers, n_phys, 2 * WIDTH, page)
    cache_i = cache_a_idx.transpose(0, 1, 3, 2)
    seq_tiles = s // tm_p
    map_p = lambda i: (i // seq_tiles, 0, 0)
    map_s = lambda i: (0, i, 0)

    xp, xs = x_prompt.reshape(n_p, d), x_sample.reshape(n_s, d)
    state_p, state_s = [], []
    for l in range(depth):
        mod = _ada(jnp.concatenate([c_prompt, c_sample], axis=0), w_ada[l], b_ada[l]).reshape(b + db, 6, d)
        mod_p = [mod[:b, k][:, None, :] for k in range(6)]
        mod_s = [jnp.repeat(mod[b:, k], t, axis=0)[None] for k in range(6)]

        o = 0
        cols = []
        for width in (WIDTH, 2 * WIDTH, N_IDX_HEADS * IDX_DIM, IDX_DIM + N_IDX_HEADS, WIDTH, 2 * WIDTH, 2 * d):
            cols.append(w_in[l][:, o:o + width])
            o += width
        cols[3] = jnp.pad(cols[3], ((0, 0), (0, LANES - cols[3].shape[1])))
        w_proj = [c.astype(BF16) for c in (cols[0], cols[1], cols[2], cols[3], cols[4], cols[5], cols[6])]
        wr_hi = w_router[l].T.astype(BF16)
        wr_lo = (w_router[l].T - wr_hi.astype(F32)).astype(BF16)
        w_mix = [w_proj_a[l].astype(BF16), w_proj_b[l].astype(BF16), w_out[l].astype(BF16), wr_hi, wr_lo]
        wsh_gu, wsh_dn = w_sh_gu[l].astype(BF16), w_sh_down[l].astype(BF16)

        (qa, qi, qb, sm, ka, kb, ki, ga, gb, kva_t, kvb_t, ki_t, vat, vbt) = _inproj(
            xp, mod_p[1], mod_p[0], map_p, g_pre_mix[l], cos_p, sin_p, seq_tiles, w_proj, tm=tm_p, seq_tiles=seq_tiles)
        oa = _dsa_prompt(qi, qa, sm, ki, ka, vat, b=b, s=s, k_top=min(TOPK_MAX, s // 4))
        ob = _sb_prompt(qb, kb, vbt, b=b, s=s)
        x1_p, h2_p, lg_p = _outproj(oa, ob, ga, gb, xp, mod_p[2], mod_p[4], mod_p[3], map_p,
                                    g_post_mix[l], g_pre_ffn[l], w_mix, tm=tm_p)
        seq_first = lambda v: v.reshape(b, 2, N_HEADS, HEAD_DIM, s).transpose(0, 4, 1, 2, 3)
        state_p.append((seq_first(kva_t), ki_t.transpose(0, 2, 1), seq_first(kvb_t)))

        (qa, qi, qb, sm, _, _, _, ga, gb, kva, kvb) = _inproj(
            xs, mod_s[1], mod_s[0], map_s, g_pre_mix[l], cos_s, sin_s, n_s // tm_s, w_proj, tm=tm_s, seq_tiles=None)
        oa = _dsa_sample(qi.reshape(db, t, WIDTH), qa.reshape(db, t, WIDTH), sm.reshape(db, t, LANES),
                         kva.reshape(db, t, 2 * WIDTH), cache_i, cache_a, page_table, l,
                         k_top=min(TOPK_MAX, (past + t) // 4))
        ob = _sb_sample(qb.reshape(db, t, WIDTH), kvb.reshape(db, t, 2 * WIDTH), cache_b, page_table, l)
        x1_s, h2_s, lg_s = _outproj(oa.reshape(n_s, WIDTH), ob.reshape(n_s, WIDTH), ga, gb, xs, mod_s[2], mod_s[4],
                                    mod_s[3], map_s, g_post_mix[l], g_pre_ffn[l], w_mix, tm=tm_s)
        state_s.append((kva.reshape(db, t, 2, N_HEADS, HEAD_DIM), sm[:, :IDX_DIM].reshape(db, t, IDX_DIM),
                        kvb.reshape(db, t, 2, N_HEADS, HEAD_DIM)))

        n_all = n_p + n_s
        br = MOE_BLOCK_ROWS
        n_blocks = -(-(n_all * EXPERT_TOP_K + n_experts * (br - 1)) // br)
        eidx, gw, rank, counts = _route(jnp.concatenate([lg_p, lg_s], axis=1), b_router[l], tn=tm_p)
        pad_start, block_expert, block_valid = _expert_layout(counts, n_blocks, br)
        dest = _moe_dest(eidx, rank, pad_start, tn=tm_p)
        x_sorted = _sc_scatter_rows(jnp.concatenate([h2_p, h2_s], axis=0), dest, n_blocks * br)
        y_sorted = _experts(x_sorted, block_expert, block_valid, w_exp_gu, w_exp_down, l)
        y_tok = _sc_gather_rows(y_sorted, dest.reshape(-1)).reshape(EXPERT_TOP_K, n_all, d // 2)
        gw_t = gw.T
        xp = _ffn_out(h2_p, y_tok, gw_t, x1_p, mod_p[5], map_p, g_post_ffn[l], wsh_gu, wsh_dn, tm=tm_p)
        xs = _ffn_out(h2_s, y_tok, gw_t, x1_s, mod_s[5], map_s, g_post_ffn[l], wsh_gu, wsh_dn, tm=tm_s, row0=n_p)

    stack = lambda states, k: jnp.stack([st[k] for st in states])
    return (xp.reshape(b, s, d), xs.reshape(db, t, d), stack(state_p, 0), stack(state_p, 1), stack(state_p, 2),
            stack(state_s, 0), stack(state_s, 1), stack(state_s, 2))
```
